```python
import math
import jax
import jax.numpy as jnp
from jax import lax
import numpy as np

D_MODEL = 1024
BATCH = 8
SEQ = 4096
DEPTH = 4

GRID_W = 64
CTX_LEN = 256
ROPE_THETA = 10000.0
Q_BLOCK = 128
EPS = 1e-6

CHUNK = 128
A_GROUPS = 4
A_GROUP_DIM = D_MODEL // 8
D_A = A_GROUPS * A_GROUP_DIM
B_HEADS = 8
B_KV_HEADS = 2
B_HEAD_DIM = 128
EV_A_END = 2 * D_A
EV_Q_END = EV_A_END + B_HEADS * B_HEAD_DIM
EV_K_END = EV_Q_END + B_KV_HEADS * B_HEAD_DIM
EV_IN = EV_K_END + B_KV_HEADS * B_HEAD_DIM
EV_OUT = D_A + B_HEADS * B_HEAD_DIM

D_C = D_MODEL // 2
CONV_W = 31
MLA_HEADS = 8
MLA_NOPE = 128
MLA_ROPE = 64
MLA_V = 128
Q_LORA = D_MODEL // 2
KV_LORA = D_MODEL // 4
OD_C_END = 2 * D_C
OD_Q_END = OD_C_END + Q_LORA
OD_IN = OD_Q_END + KV_LORA + MLA_ROPE
OD_OUT = D_C + MLA_HEADS * MLA_V

D_FF = 2816
FFN_CONV_W = 3

kernel_name = "hybrid_gmlp_gqa_conformer_mla_convffn_dit"


def rmsnorm(x, g):
    xf = x.astype(jnp.float32)
    y = xf * lax.rsqrt(jnp.mean(xf * xf, axis=-1, keepdims=True) + EPS)
    return (y * g.astype(jnp.float32)).astype(x.dtype)


def layernorm(x, g, b):
    xf = x.astype(jnp.float32)
    mu = jnp.mean(xf, axis=-1, keepdims=True)
    xc = xf - mu
    y = xc * lax.rsqrt(jnp.mean(xc * xc, axis=-1, keepdims=True) + EPS)
    return (y * g.astype(jnp.float32) + b.astype(jnp.float32)).astype(x.dtype)


def dwconv(x, w, b):
    k = w.shape[0]
    y = lax.conv_general_dilated(
        x, w[:, None, :].astype(x.dtype), window_strides=(1,),
        padding=[(k // 2, k // 2)], dimension_numbers=('NWC', 'WIO', 'NWC'),
        feature_group_count=x.shape[-1])
    return y + b.astype(x.dtype)


def axial_rope_tables(row, col, d_rot):
    n_freq = d_rot // 4
    freq = ROPE_THETA ** (-jnp.arange(n_freq, dtype=jnp.float32) / n_freq)
    ang = jnp.concatenate([row[:, None] * freq, col[:, None] * freq], axis=-1)
    return jnp.cos(ang), jnp.sin(ang)


def apply_rope(x, cos, sin):
    b, n, h, d = x.shape
    xp = x.reshape(b, n, h, d // 2, 2)
    x0, x1 = xp[..., 0], xp[..., 1]
    cs = cos[None, :, None, :].astype(x.dtype)
    sn = sin[None, :, None, :].astype(x.dtype)
    return jnp.stack([x0 * cs - x1 * sn, x0 * sn + x1 * cs], axis=-1).reshape(b, n, h, d)


def block_attention(q, k, v, scale):
    b, n, g, r, dq = q.shape
    nb = n // Q_BLOCK
    qb = jnp.moveaxis(q.reshape(b, nb, Q_BLOCK, g, r, dq), 1, 0)

    def one_block(q_blk):
        s = jnp.einsum('bqgrd,bkgd->bgrqk', q_blk, k, preferred_element_type=jnp.float32) * scale
        p = jax.nn.softmax(s, axis=-1).astype(v.dtype)
        return jnp.einsum('bgrqk,bkgd->bqgrd', p, v)

    o = lax.map(one_block, qb)
    return jnp.moveaxis(o, 0, 1).reshape(b, n, g * r * v.shape[-1])


def spatial_gating(z, ln_g, ln_b, w_s, b_s):
    z = jax.nn.gelu(z)
    u, v = jnp.split(z, 2, axis=-1)
    v = layernorm(v, ln_g, ln_b)
    b, n, _ = v.shape
    vb = v.reshape(b, n // CHUNK, CHUNK, A_GROUPS, A_GROUP_DIM)
    s = jnp.einsum('gpq,bnqgd->bnpgd', w_s, vb) + b_s.T[:, :, None]
    return u * s.reshape(b, n, D_A)


def conformer_conv(z, conv_w, conv_b, ln_g, ln_b):
    a, gt = jnp.split(z, 2, axis=-1)
    h = dwconv(a * jax.nn.sigmoid(gt), conv_w, conv_b)
    return jax.nn.silu(layernorm(h, ln_g, ln_b))


def mla_q(z_q, g_cq, w_uq, cos, sin):
    b, n, _ = z_q.shape
    q = (rmsnorm(z_q, g_cq) @ w_uq).reshape(b, n, MLA_HEADS, MLA_NOPE + MLA_ROPE)
    if cos is None:
        return q
    return jnp.concatenate([q[..., :MLA_NOPE], apply_rope(q[..., MLA_NOPE:], cos, sin)], axis=-1)


def mla_kv(z_kv, g_ckv, w_ukv, cos, sin):
    b, n, _ = z_kv.shape
    kv = (rmsnorm(z_kv[..., :KV_LORA], g_ckv) @ w_ukv).reshape(b, n, MLA_HEADS, MLA_NOPE + MLA_V)
    k_rope = z_kv[..., KV_LORA:].reshape(b, n, 1, MLA_ROPE)
    if cos is not None:
        k_rope = apply_rope(k_rope, cos, sin)
    k = jnp.concatenate([kv[..., :MLA_NOPE], jnp.broadcast_to(k_rope, (b, n, MLA_HEADS, MLA_ROPE))], axis=-1)
    return k, kv[..., MLA_NOPE:]


def even_mixer(h_lat, h_ctx, w_in, a_ln, w_s, b_s, qk_g, w_out, cos, sin, with_ctx_out):
    b, n, _ = h_lat.shape
    l = h_ctx.shape[1]
    rep = B_HEADS // B_KV_HEADS
    scale = B_HEAD_DIM ** -0.5
    kvw = B_KV_HEADS * B_HEAD_DIM
    zc_kv = h_ctx @ w_in[:, EV_Q_END:]
    k_ctx = rmsnorm(zc_kv[..., :kvw].reshape(b, l, B_KV_HEADS, B_HEAD_DIM), qk_g[1])
    v_ctx = zc_kv[..., kvw:].reshape(b, l, B_KV_HEADS, B_HEAD_DIM)
    z = h_lat @ w_in
    a_lat = spatial_gating(z[..., :EV_A_END], a_ln[0], a_ln[1], w_s, b_s)
    q = apply_rope(rmsnorm(z[..., EV_A_END:EV_Q_END].reshape(b, n, B_HEADS, B_HEAD_DIM), qk_g[0]), cos, sin)
    k = apply_rope(rmsnorm(z[..., EV_Q_END:EV_K_END].reshape(b, n, B_KV_HEADS, B_HEAD_DIM), qk_g[1]), cos, sin)
    v = z[..., EV_K_END:].reshape(b, n, B_KV_HEADS, B_HEAD_DIM)
    att = block_attention(q.reshape(b, n, B_KV_HEADS, rep, B_HEAD_DIM),
                          jnp.concatenate([k_ctx, k], axis=1), jnp.concatenate([v_ctx, v], axis=1), scale)
    o_lat = jnp.concatenate([a_lat, att], axis=-1) @ w_out
    if not with_ctx_out:
        return o_lat, None
    zc = h_ctx @ w_in[:, :EV_Q_END]
    a_ctx = spatial_gating(zc[..., :EV_A_END], a_ln[0], a_ln[1], w_s, b_s)
    q_ctx = rmsnorm(zc[..., EV_A_END:].reshape(b, l, B_HEADS, B_HEAD_DIM), qk_g[0])
    att_ctx = block_attention(q_ctx.reshape(b, l, B_KV_HEADS, rep, B_HEAD_DIM), k_ctx, v_ctx, scale)
    o_ctx = jnp.concatenate([a_ctx, att_ctx], axis=-1) @ w_out
    return o_lat, o_ctx


def odd_mixer(h_lat, h_ctx, w_in, conv_w, conv_b, c_ln, g_cq, g_ckv, w_uq, w_ukv, w_out, cos, sin, with_ctx_out):
    scale = (MLA_NOPE + MLA_ROPE) ** -0.5
    k_ctx, v_ctx = mla_kv(h_ctx @ w_in[:, OD_Q_END:], g_ckv, w_ukv, None, None)
    z = h_lat @ w_in
    c_lat = conformer_conv(z[..., :OD_C_END], conv_w, conv_b, c_ln[0], c_ln[1])
    q = mla_q(z[..., OD_C_END:OD_Q_END], g_cq, w_uq, cos, sin)
    k, v = mla_kv(z[..., OD_Q_END:], g_ckv, w_ukv, cos, sin)
    att = block_attention(q[:, :, :, None, :], jnp.concatenate([k_ctx, k], axis=1),
                          jnp.concatenate([v_ctx, v], axis=1), scale)
    o_lat = jnp.concatenate([c_lat, att], axis=-1) @ w_out
    if not with_ctx_out:
        return o_lat, None
    zc = h_ctx @ w_in[:, :OD_Q_END]
    c_ctx_o = conformer_conv(zc[..., :OD_C_END], conv_w, conv_b, c_ln[0], c_ln[1])
    q_ctx = mla_q(zc[..., OD_C_END:], g_cq, w_uq, None, None)
    att_ctx = block_attention(q_ctx[:, :, :, None, :], k_ctx, v_ctx, scale)
    o_ctx = jnp.concatenate([c_ctx_o, att_ctx], axis=-1) @ w_out
    return o_lat, o_ctx


def conv_ffn(h, w_in, conv_w, conv_b, w_out):
    z = h @ w_in
    g = dwconv(z[..., :D_FF], conv_w, conv_b)
    return (jax.nn.silu(g) * z[..., D_FF:]) @ w_out


def setup_inputs(seed: int = 0) -> dict:
    key = jax.random.key(seed)
    ks = iter(jax.random.split(key, 40))
    ne, no = (DEPTH + 1) // 2, DEPTH // 2

    def nrm(shape, scale):
        return jax.random.normal(next(ks), shape, jnp.float32) * scale

    def gain(shape):
        return 1.0 + nrm(shape, 0.05)

    return dict(
        x=nrm((BATCH, SEQ, D_MODEL), 1.0),
        c=nrm((BATCH, D_MODEL), 1.0),
        ctx=nrm((BATCH, CTX_LEN, D_MODEL), 1.0),
        c_ctx=nrm((D_MODEL,), 1.0),
        mod_w=nrm((DEPTH, D_MODEL, 6 * D_MODEL), 0.5 * D_MODEL ** -0.5),
        mod_b=nrm((DEPTH, 6 * D_MODEL), 0.02),
        norm_g=gain((DEPTH, 4, D_MODEL)),
        ffn_w_in=nrm((DEPTH, D_MODEL, 2 * D_FF), D_MODEL ** -0.5),
        ffn_conv_w=nrm((DEPTH, FFN_CONV_W, D_FF), FFN_CONV_W ** -0.5),
        ffn_conv_b=nrm((DEPTH, D_FF), 0.02),
        ffn_w_out=nrm((DEPTH, D_FF, D_MODEL), D_FF ** -0.5),
        ev_w_in=nrm((ne, D_MODEL, EV_IN), D_MODEL ** -0.5),
        ev_a_ln=jnp.stack([gain((ne, D_A)), nrm((ne, D_A), 0.02)], axis=1),
        ev_w_s=nrm((ne, A_GROUPS, CHUNK, CHUNK), CHUNK ** -0.5),
        ev_b_s=1.0 + nrm((ne, A_GROUPS, CHUNK), 0.1),
        ev_qk_g=gain((ne, 2, B_HEAD_DIM)),
        ev_w_out=nrm((ne, EV_OUT, D_MODEL), EV_OUT ** -0.5),
        od_w_in=nrm((no, D_MODEL, OD_IN), D_MODEL ** -0.5),
        od_conv_w=nrm((no, CONV_W, D_C), CONV_W ** -0.5),
        od_conv_b=nrm((no, D_C), 0.02),
        od_c_ln=jnp.stack([gain((no, D_C)), nrm((no, D_C), 0.02)], axis=1),
        od_g_cq=gain((no, Q_LORA)),
        od_g_ckv=gain((no, KV_LORA)),
        od_w_uq=nrm((no, Q_LORA, MLA_HEADS * (MLA_NOPE + MLA_ROPE)), Q_LORA ** -0.5),
        od_w_ukv=nrm((no, KV_LORA, MLA_HEADS * (MLA_NOPE + MLA_V)), KV_LORA ** -0.5),
        od_w_out=nrm((no, OD_OUT, D_MODEL), OD_OUT ** -0.5),
    )


def reference(x, c, ctx, c_ctx, mod_w, mod_b, norm_g, ffn_w_in, ffn_conv_w, ffn_conv_b, ffn_w_out,
              ev_w_in, ev_a_ln, ev_w_s, ev_b_s, ev_qk_g, ev_w_out,
              od_w_in, od_conv_w, od_conv_b, od_c_ln, od_g_cq, od_g_ckv, od_w_uq, od_w_ukv, od_w_out):
    n = x.shape[1]
    rows = n // GRID_W
    row = jnp.repeat(jnp.arange(rows, dtype=jnp.float32), GRID_W)
    col = jnp.tile(jnp.arange(GRID_W, dtype=jnp.float32), rows)
    cos_b, sin_b = axial_rope_tables(row, col, B_HEAD_DIM)
    cos_d, sin_d = axial_rope_tables(row, col, MLA_ROPE)
    s_c = jax.nn.silu(c)
    s_cc = jax.nn.silu(c_ctx)
    x_lat, x_ctx = x, ctx
    for layer in range(DEPTH):
        more = layer < DEPTH - 1
        m_lat = jnp.split((s_c @ mod_w[layer] + mod_b[layer])[:, None, :], 6, axis=-1)
        m_ctx = jnp.split(s_cc @ mod_w[layer] + mod_b[layer], 6, axis=-1)
        g_pre1, g_post1, g_pre2, g_post2 = norm_g[layer]
        h_lat = rmsnorm(x_lat, g_pre1) * (1 + m_lat[1]) + m_lat[0]
        h_ctx = rmsnorm(x_ctx, g_pre1) * (1 + m_ctx[1]) + m_ctx[0]
        i = layer // 2
        if layer % 2 == 0:
            o_lat, o_ctx = even_mixer(h_lat, h_ctx, ev_w_in[i], ev_a_ln[i], ev_w_s[i], ev_b_s[i],
                                      ev_qk_g[i], ev_w_out[i], cos_b, sin_b, more)
        else:
            o_lat, o_ctx = odd_mixer(h_lat, h_ctx, od_w_in[i], od_conv_w[i], od_conv_b[i], od_c_ln[i],
                                     od_g_cq[i], od_g_ckv[i], od_w_uq[i], od_w_ukv[i], od_w_out[i],
                                     cos_d, sin_d, more)
        x_lat = x_lat + m_lat[2] * rmsnorm(o_lat, g_post1)
        f_lat = conv_ffn(rmsnorm(x_lat, g_pre2) * (1 + m_lat[4]) + m_lat[3],
                         ffn_w_in[layer], ffn_conv_w[layer], ffn_conv_b[layer], ffn_w_out[layer])
        x_lat = x_lat + m_lat[5] * rmsnorm(f_lat, g_post2)
        if more:
            x_ctx = x_ctx + m_ctx[2] * rmsnorm(o_ctx, g_post1)
            f_ctx = conv_ffn(rmsnorm(x_ctx, g_pre2) * (1 + m_ctx[4]) + m_ctx[3],
                             ffn_w_in[layer], ffn_conv_w[layer], ffn_conv_b[layer], ffn_w_out[layer])
            x_ctx = x_ctx + m_ctx[5] * rmsnorm(f_ctx, g_post2)
    return x_lat
```

```python
import functools

import jax
import jax.numpy as jnp
import numpy as np
from jax import lax
from jax.experimental import pallas as pl
from jax.experimental.pallas import tpu as pltpu

F32 = jnp.float32
BF16 = jnp.bfloat16

GRID_W = 64
ROPE_THETA = 10000.0
EPS = 1e-6

CHUNK = 128
A_GROUPS = 4
A_GROUP_DIM = 128
D_A = A_GROUPS * A_GROUP_DIM
B_HEADS = 8
B_KV_HEADS = 2
B_HEAD_DIM = 128
EV_A_END = 2 * D_A
EV_Q_END = EV_A_END + B_HEADS * B_HEAD_DIM
EV_K_END = EV_Q_END + B_KV_HEADS * B_HEAD_DIM
EV_IN = EV_K_END + B_KV_HEADS * B_HEAD_DIM

D_C = 512
CONV_W = 31
MLA_HEADS = 8
MLA_NOPE = 128
MLA_ROPE = 64
MLA_V = 128
MLA_QK_PAD = 256
Q_LORA = 512
KV_LORA = 256
OD_C_END = 2 * D_C
OD_Q_END = OD_C_END + Q_LORA
OD_KV_END = OD_Q_END + KV_LORA
OD_IN_PAD = OD_KV_END + 128

D_FF = 2816
FFN_CHUNK = 256

CONV_HALO = 16
FFN_HALO = 8

V7X_VMEM_LIMIT = 56 * 1024 * 1024


def _rms(xf, g):
    return xf * lax.rsqrt(jnp.mean(xf * xf, axis=-1, keepdims=True) + EPS) * g


def _layernorm(xf, g, b):
    mu = jnp.mean(xf, axis=-1, keepdims=True)
    xc = xf - mu
    return xc * lax.rsqrt(jnp.mean(xc * xc, axis=-1, keepdims=True) + EPS) * g + b


def _sigmoid(x):
    return 1.0 / (1.0 + jnp.exp(-x))


def _gelu_tanh(x):
    return 0.5 * x * (1.0 + jnp.tanh(np.float32(np.sqrt(2.0 / np.pi)) * (x + 0.044715 * (x * x * x))))


def _dot(a, b):
    return jnp.dot(a, b, preferred_element_type=F32)


def _params(sem):
    return pltpu.CompilerParams(dimension_semantics=sem, vmem_limit_bytes=V7X_VMEM_LIMIT)


def _const_spec(shape):
    nd = len(shape)
    return pl.BlockSpec(shape, lambda *_: (0,) * nd, pipeline_mode=pl.Buffered(1))


def _tok_spec(tm, width):
    return pl.BlockSpec((None, tm, width), lambda b, i: (b, i, 0))


def _mod_spec(mod):
    d = mod.shape[-1]
    if mod.shape[0] == 1:
        return pl.BlockSpec((None, 6, d), lambda b, i: (0, 0, 0))
    return pl.BlockSpec((None, 6, d), lambda b, i: (b, 0, 0))


def _halo_specs(tm, halo, n, width):
    r = tm // halo
    last = n // halo - 1
    prev = pl.BlockSpec((None, halo, width), lambda b, i: (b, jnp.maximum(i * r - 1, 0), 0))
    nxt = pl.BlockSpec((None, halo, width), lambda b, i: (b, jnp.minimum((i + 1) * r, last), 0))
    return prev, nxt


def _mod_kernel(s_ref, w_ref, b_ref, o_ref):
    s = s_ref[...]
    s = s * _sigmoid(s)
    o_ref[...] = jnp.dot(s, w_ref[...], preferred_element_type=F32,
                         precision=lax.Precision.HIGHEST) + b_ref[...]


def _modulation(cc, mod_w, mod_b):
    depth, d, d6 = mod_w.shape
    r = cc.shape[0]
    out = pl.pallas_call(
        _mod_kernel,
        grid=(depth, d6 // d),
        in_specs=[pl.BlockSpec((r, d), lambda l, j: (0, 0)),
                  pl.BlockSpec((None, d, d), lambda l, j: (l, 0, j)),
                  pl.BlockSpec((None, 1, d), lambda l, j: (l, 0, j))],
        out_specs=pl.BlockSpec((None, r, d), lambda l, j: (l, 0, j)),
        out_shape=jax.ShapeDtypeStruct((depth, r, d6), F32),
        compiler_params=_params(("arbitrary", "arbitrary")),
        name="modulation",
    )(cc, mod_w, mod_b.reshape(depth, 1, d6))
    return out.reshape(depth, r, 6, d)


def _ev_in_kernel(x_ref, mod_ref, g_ref, w_ref, lng_ref, lnb_ref, ws_ref, bs_ref, gq_ref, gk_ref,
                  cos_ref, sin_ref, a_ref, q_ref, k_ref, v_ref, *, tm):
    h = _rms(x_ref[...], g_ref[...]) * (1.0 + mod_ref[1:2, :]) + mod_ref[0:1, :]
    z = _dot(h.astype(BF16), w_ref[...])

    ga = _gelu_tanh(z[:, :EV_A_END])
    u = ga[:, :D_A]
    vv = _layernorm(ga[:, D_A:], lng_ref[...], lnb_ref[...]).astype(BF16)
    for c in range(tm // CHUNK):
        rows = slice(c * CHUNK, (c + 1) * CHUNK)
        for g in range(A_GROUPS):
            cols = slice(g * A_GROUP_DIM, (g + 1) * A_GROUP_DIM)
            s = _dot(ws_ref[g], vv[rows, cols]) + bs_ref[g]
            a_ref[rows, cols] = (u[rows, cols] * s).astype(a_ref.dtype)

    cosf = cos_ref[...]
    sinf = sin_ref[...]

    def head(zh, g):
        y = _rms(zh, g)
        return y * cosf + pltpu.roll(y, B_HEAD_DIM // 2, 1) * sinf

    scale = np.float32(B_HEAD_DIM ** -0.5)
    for hd in range(B_HEADS):
        cols = slice(hd * B_HEAD_DIM, (hd + 1) * B_HEAD_DIM)
        zh = z[:, EV_A_END + hd * B_HEAD_DIM:EV_A_END + (hd + 1) * B_HEAD_DIM]
        q_ref[:, cols] = (head(zh, gq_ref[...]) * scale).astype(q_ref.dtype)
    for hd in range(B_KV_HEADS):
        cols = slice(hd * B_HEAD_DIM, (hd + 1) * B_HEAD_DIM)
        zh = z[:, EV_Q_END + hd * B_HEAD_DIM:EV_Q_END + (hd + 1) * B_HEAD_DIM]
        k_ref[:, cols] = head(zh, gk_ref[...]).astype(k_ref.dtype)
    v_ref[...] = z[:, EV_K_END:].astype(v_ref.dtype)


def _ev_in(x, mod, g_pre, w_in, ln_g, ln_b, w_s, b_s, gq, gk, cosf, sinf, tm):
    b, n, d = x.shape
    kvw = B_KV_HEADS * B_HEAD_DIM
    qw = B_HEADS * B_HEAD_DIM
    return pl.pallas_call(
        functools.partial(_ev_in_kernel, tm=tm),
        grid=(b, n // tm),
        in_specs=[_tok_spec(tm, d), _mod_spec(mod), _const_spec((1, d)), _const_spec(w_in.shape),
                  _const_spec((1, D_A)), _const_spec((1, D_A)), _const_spec(w_s.shape), _const_spec(b_s.shape),
                  _const_spec((1, B_HEAD_DIM)), _const_spec((1, B_HEAD_DIM)),
                  pl.BlockSpec((tm, B_HEAD_DIM), lambda bb, i: (i, 0)),
                  pl.BlockSpec((tm, B_HEAD_DIM), lambda bb, i: (i, 0))],
        out_specs=[_tok_spec(tm, D_A), _tok_spec(tm, qw), _tok_spec(tm, kvw), _tok_spec(tm, kvw)],
        out_shape=[jax.ShapeDtypeStruct((b, n, D_A), BF16), jax.ShapeDtypeStruct((b, n, qw), BF16),
                   jax.ShapeDtypeStruct((b, n, kvw), BF16), jax.ShapeDtypeStruct((b, n, kvw), BF16)],
        compiler_params=_params(("parallel", "parallel")),
        name="ev_in",
    )(x, mod, g_pre, w_in, ln_g, ln_b, w_s, b_s, gq, gk, cosf, sinf)


def _od_in_kernel(x_ref, mod_ref, g_ref, w_ref, gcq_ref, gckv_ref, wuq_ref, wukv_ref,
                  cos_ref, sa_ref, sb_ref, glu_ref, q_ref, k_ref, v_ref):
    h = _rms(x_ref[...], g_ref[...]) * (1.0 + mod_ref[1:2, :]) + mod_ref[0:1, :]
    z = _dot(h.astype(BF16), w_ref[...])

    glu_ref[...] = z[:, :D_C] * _sigmoid(z[:, D_C:OD_C_END])

    cosf = cos_ref[...]
    sa = sa_ref[...]
    sb = sb_ref[...]

    def rope(y):
        return y * cosf + pltpu.roll(y, MLA_ROPE // 2, 1) * sa + pltpu.roll(y, 128 - MLA_ROPE // 2, 1) * sb

    scale = np.float32((MLA_NOPE + MLA_ROPE) ** -0.5)
    cq = _rms(z[:, OD_C_END:OD_Q_END], gcq_ref[...]).astype(BF16)
    qp = _dot(cq, wuq_ref[...])
    ckv = _rms(z[:, OD_Q_END:OD_KV_END], gckv_ref[...]).astype(BF16)
    kv = _dot(ckv, wukv_ref[...])
    kr = rope(z[:, OD_KV_END:OD_IN_PAD]).astype(k_ref.dtype)
    for hd in range(MLA_HEADS):
        lo = hd * MLA_QK_PAD
        mid = lo + MLA_NOPE
        hi = lo + MLA_QK_PAD
        q_ref[:, lo:mid] = (qp[:, lo:mid] * scale).astype(q_ref.dtype)
        q_ref[:, mid:hi] = (rope(qp[:, mid:hi]) * scale).astype(q_ref.dtype)
        k_ref[:, lo:mid] = kv[:, hd * MLA_NOPE:(hd + 1) * MLA_NOPE].astype(k_ref.dtype)
        k_ref[:, mid:hi] = kr
    v_ref[...] = kv[:, MLA_HEADS * MLA_NOPE:].astype(v_ref.dtype)


def _od_in(x, mod, g_pre, w_in, g_cq, g_ckv, w_uq, w_ukv, cosf, sa, sb, tm):
    b, n, d = x.shape
    qkw = MLA_HEADS * MLA_QK_PAD
    vw = MLA_HEADS * MLA_V
    tab = pl.BlockSpec((tm, 128), lambda bb, i: (i, 0))
    return pl.pallas_call(
        _od_in_kernel,
        grid=(b, n // tm),
        in_specs=[_tok_spec(tm, d), _mod_spec(mod), _const_spec((1, d)), _const_spec(w_in.shape),
                  _const_spec((1, Q_LORA)), _const_spec((1, KV_LORA)), _const_spec(w_uq.shape),
                  _const_spec(w_ukv.shape), tab, tab, tab],
        out_specs=[_tok_spec(tm, D_C), _tok_spec(tm, qkw), _tok_spec(tm, qkw), _tok_spec(tm, vw)],
        out_shape=[jax.ShapeDtypeStruct((b, n, D_C), F32), jax.ShapeDtypeStruct((b, n, qkw), BF16),
                   jax.ShapeDtypeStruct((b, n, qkw), BF16), jax.ShapeDtypeStruct((b, n, vw), BF16)],
        compiler_params=_params(("parallel", "parallel")),
        name="od_in",
    )(x, mod, g_pre, w_in, g_cq, g_ckv, w_uq, w_ukv, cosf, sa, sb)


def _attn_kernel(*refs, nseg, rep, dq, dv):
    q_ref = refs[0]
    k_refs = refs[1:1 + nseg]
    v_refs = refs[1 + nseg:1 + 2 * nseg]
    o_ref = refs[1 + 2 * nseg]
    for r in range(rep):
        q = q_ref[:, r * dq:(r + 1) * dq]
        s = [lax.dot_general(q, k[...], (((1,), (1,)), ((), ())), preferred_element_type=F32) for k in k_refs]
        m = functools.reduce(jnp.maximum, [jnp.max(si, axis=-1, keepdims=True) for si in s])
        p = [jnp.exp(si - m) for si in s]
        l = functools.reduce(jnp.add, [jnp.sum(pi, axis=-1, keepdims=True) for pi in p])
        o = functools.reduce(jnp.add, [_dot(pi.astype(BF16), v[...]) for pi, v in zip(p, v_refs)])
        o_ref[:, r * dv:(r + 1) * dv] = (o / l).astype(o_ref.dtype)


def _attention(q, ks, vs, groups, rep, dq, dv, tq):
    b, n, _ = q.shape
    nseg = len(ks)
    kv_specs = ([pl.BlockSpec((None, k.shape[1], dq), lambda bb, g, i: (bb, 0, g)) for k in ks]
                + [pl.BlockSpec((None, v.shape[1], dv), lambda bb, g, i: (bb, 0, g)) for v in vs])
    return pl.pallas_call(
        functools.partial(_attn_kernel, nseg=nseg, rep=rep, dq=dq, dv=dv),
        grid=(b, groups, n // tq),
        in_specs=[pl.BlockSpec((None, tq, rep * dq), lambda bb, g, i: (bb, i, g))] + kv_specs,
        out_specs=pl.BlockSpec((None, tq, rep * dv), lambda bb, g, i: (bb, i, g)),
        out_shape=jax.ShapeDtypeStruct((b, n, groups * rep * dv), BF16),
        compiler_params=_params(("parallel", "parallel", "arbitrary")),
        name="attention",
    )(q, *ks, *vs)


def _conformer_tail(main_ref, prev_ref, next_ref, cw_ref, cb_ref, lng_ref, lnb_ref, tm):
    i = pl.program_id(1)
    prev = jnp.where(i > 0, prev_ref[...], 0.0)
    nxt = jnp.where(i < pl.num_programs(1) - 1, next_ref[...], 0.0)
    ext = jnp.concatenate([prev, main_ref[...], nxt], axis=0)
    half = CONV_W // 2
    acc = None
    for r in range(8):
        rolled = ext if r == 0 else pltpu.roll(ext, r, 0)
        for a in range(-2, 2):
            tap = half - (8 * a + r)
            if 0 <= tap < CONV_W:
                lo = CONV_HALO - 8 * a
                term = cw_ref[tap:tap + 1, :] * rolled[lo:lo + tm]
                acc = term if acc is None else acc + term
    hc = _layernorm(acc + cb_ref[...], lng_ref[...], lnb_ref[...])
    return hc * _sigmoid(hc)


def _proj_kernel(*refs, conv, tm, w1):
    if conv:
        (x_ref, mod_ref, g_ref, b1_ref, b1p_ref, b1n_ref, att_ref, w_ref,
         cw_ref, cb_ref, lng_ref, lnb_ref, o_ref) = refs
        b1 = _conformer_tail(b1_ref, b1p_ref, b1n_ref, cw_ref, cb_ref, lng_ref, lnb_ref, tm).astype(BF16)
    else:
        x_ref, mod_ref, g_ref, b1_ref, att_ref, w_ref, o_ref = refs
        b1 = b1_ref[...]
    o = _dot(b1, w_ref[:w1, :]) + _dot(att_ref[...], w_ref[w1:, :])
    o_ref[...] = x_ref[...] + mod_ref[2:3, :] * _rms(o, g_ref[...])


def _proj(x, mod, g_post, b1, att, w_out, tm, conv_params=None):
    b, n, d = x.shape
    w1 = b1.shape[-1]
    conv = conv_params is not None
    in_specs = [_tok_spec(tm, d), _mod_spec(mod), _const_spec((1, d)), _tok_spec(tm, w1)]
    args = [x, mod, g_post, b1]
    if conv:
        in_specs += list(_halo_specs(tm, CONV_HALO, n, w1))
        args += [b1, b1]
    in_specs += [_tok_spec(tm, att.shape[-1]), _const_spec(w_out.shape)]
    args += [att, w_out]
    if conv:
        in_specs += [_const_spec(p.shape) for p in conv_params]
        args += list(conv_params)
    return pl.pallas_call(
        functools.partial(_proj_kernel, conv=conv, tm=tm, w1=w1),
        grid=(b, n // tm),
        in_specs=in_specs,
        out_specs=_tok_spec(tm, d),
        out_shape=jax.ShapeDtypeStruct((b, n, d), F32),
        compiler_params=_params(("parallel", "parallel")),
        name="proj_conv" if conv else "proj",
    )(*args)


def _ffn_kernel(x_ref, xp_ref, xn_ref, mod_ref, g2_ref, gp2_ref, win_ref, cw_ref, cb_ref, wout_ref,
                o_ref, acc_ref, *, tm):
    i = pl.program_id(1)
    shift = mod_ref[3:4, :]
    scale1 = 1.0 + mod_ref[4:5, :]

    def pre(xx):
        return _rms(xx, g2_ref[...]) * scale1 + shift

    x = x_ref[...]
    hm = pre(x)
    hp = jnp.where(i > 0, pre(xp_ref[...]), 0.0)
    hn = jnp.where(i < pl.num_programs(1) - 1, pre(xn_ref[...]), 0.0)
    hext = jnp.concatenate([hp, hm, hn], axis=0).astype(BF16)
    hmb = hm.astype(BF16)
    rows = tm + 2 * FFN_HALO
    for c in range(D_FF // FFN_CHUNK):
        cols = slice(c * FFN_CHUNK, (c + 1) * FFN_CHUNK)
        zg = _dot(hext, win_ref[:, cols])
        zu = _dot(hmb, win_ref[:, D_FF + c * FFN_CHUNK:D_FF + (c + 1) * FFN_CHUNK])
        g = (cw_ref[0:1, cols] * pltpu.roll(zg, 1, 0)[FFN_HALO:FFN_HALO + tm]
             + cw_ref[1:2, cols] * zg[FFN_HALO:FFN_HALO + tm]
             + cw_ref[2:3, cols] * pltpu.roll(zg, rows - 1, 0)[FFN_HALO:FFN_HALO + tm]
             + cb_ref[:, cols])
        act = (g * _sigmoid(g) * zu).astype(BF16)
        contrib = _dot(act, wout_ref[cols, :])
        if c == 0:
            acc_ref[...] = contrib
        else:
            acc_ref[...] += contrib
    o_ref[...] = x + mod_ref[5:6, :] * _rms(acc_ref[...], gp2_ref[...])


def _ffn(x, mod, g_pre2, g_post2, w_in, conv_w, conv_b, w_out, tm):
    b, n, d = x.shape
    prev, nxt = _halo_specs(tm, FFN_HALO, n, d)
    return pl.pallas_call(
        functools.partial(_ffn_kernel, tm=tm),
        grid=(b, n // tm),
        in_specs=[_tok_spec(tm, d), prev, nxt, _mod_spec(mod), _const_spec((1, d)), _const_spec((1, d)),
                  _const_spec(w_in.shape), _const_spec(conv_w.shape), _const_spec(conv_b.shape),
                  _const_spec(w_out.shape)],
        out_specs=_tok_spec(tm, d),
        out_shape=jax.ShapeDtypeStruct((b, n, d), F32),
        scratch_shapes=[pltpu.VMEM((tm, d), F32)],
        compiler_params=_params(("parallel", "parallel")),
        name="conv_ffn",
    )(x, x, x, mod, g_pre2, g_post2, w_in, conv_w, conv_b, w_out)


def _deinterleave(d):
    return np.concatenate([np.arange(0, d, 2), np.arange(1, d, 2)])


def _rope_angles(n, d_rot):
    t = jnp.arange(n)
    row = (t // GRID_W).astype(F32)
    col = (t % GRID_W).astype(F32)
    n_freq = d_rot // 4
    freq = ROPE_THETA ** (-jnp.arange(n_freq, dtype=F32) / n_freq)
    ang = jnp.concatenate([row[:, None] * freq, col[:, None] * freq], axis=-1)
    return jnp.cos(ang), jnp.sin(ang)


def kernel(x, c, ctx, c_ctx, mod_w, mod_b, norm_g, ffn_w_in, ffn_conv_w, ffn_conv_b, ffn_w_out,
           ev_w_in, ev_a_ln, ev_w_s, ev_b_s, ev_qk_g, ev_w_out,
           od_w_in, od_conv_w, od_conv_b, od_c_ln, od_g_cq, od_g_ckv, od_w_uq, od_w_ukv, od_w_out):
    b, n, d = x.shape
    l_ctx = ctx.shape[1]
    depth = mod_w.shape[0]
    tm_lat, tm_ctx, tq = 512, l_ctx, 256

    n_rows = -(-(b + 1) // 8) * 8
    cc = jnp.concatenate([c, c_ctx[None, :], jnp.zeros((n_rows - b - 1, d), F32)], axis=0)
    mods = _modulation(cc, mod_w, mod_b)

    cos_b, sin_b = _rope_angles(n, B_HEAD_DIM)
    ev_cos = {"lat": jnp.concatenate([cos_b, cos_b], axis=-1), "ctx": jnp.ones((l_ctx, B_HEAD_DIM), F32)}
    ev_sin = {"lat": jnp.concatenate([-sin_b, sin_b], axis=-1), "ctx": jnp.zeros((l_ctx, B_HEAD_DIM), F32)}
    cos_d, sin_d = _rope_angles(n, MLA_ROPE)
    hr = MLA_ROPE // 2
    zpad = jnp.zeros((n, 128 - MLA_ROPE), F32)
    od_cos = {"lat": jnp.concatenate([cos_d, cos_d, zpad], axis=-1),
              "ctx": jnp.concatenate([jnp.ones((l_ctx, MLA_ROPE), F32), jnp.zeros((l_ctx, 128 - MLA_ROPE), F32)], -1)}
    od_sa = {"lat": jnp.concatenate([jnp.zeros((n, hr), F32), sin_d, zpad], axis=-1),
             "ctx": jnp.zeros((l_ctx, 128), F32)}
    od_sb = {"lat": jnp.concatenate([-sin_d, jnp.zeros((n, hr), F32), zpad], axis=-1),
             "ctx": jnp.zeros((l_ctx, 128), F32)}

    perm_b = _deinterleave(B_HEAD_DIM)
    perm_d = _deinterleave(MLA_ROPE)

    xs = {"lat": x, "ctx": ctx}
    tms = {"lat": tm_lat, "ctx": tm_ctx}
    for layer in range(depth):
        more = layer < depth - 1
        i = layer // 2
        mod = {"lat": mods[layer, :b], "ctx": mods[layer, b:b + 1]}
        g_pre1, g_post1, g_pre2, g_post2 = (norm_g[layer, j][None, :] for j in range(4))
        streams = ("lat", "ctx")
        if layer % 2 == 0:
            w = ev_w_in[i]
            wq = w[:, EV_A_END:EV_Q_END].reshape(d, B_HEADS, B_HEAD_DIM)[:, :, perm_b].reshape(d, -1)
            wk = w[:, EV_Q_END:EV_K_END].reshape(d, B_KV_HEADS, B_HEAD_DIM)[:, :, perm_b].reshape(d, -1)
            w_in = jnp.concatenate([w[:, :EV_A_END], wq, wk, w[:, EV_K_END:]], axis=1).astype(BF16)
            ln_g, ln_b = ev_a_ln[i, 0][None, :], ev_a_ln[i, 1][None, :]
            w_s = ev_w_s[i].astype(BF16)
            b_s = jnp.broadcast_to(ev_b_s[i][:, :, None], (A_GROUPS, CHUNK, A_GROUP_DIM))
            gq, gk = ev_qk_g[i, 0][perm_b][None, :], ev_qk_g[i, 1][perm_b][None, :]
            w_out = ev_w_out[i].astype(BF16)
            pr = {s: _ev_in(xs[s], mod[s], g_pre1, w_in, ln_g, ln_b, w_s, b_s, gq, gk,
                            ev_cos[s], ev_sin[s], tms[s]) for s in streams}
            geom = dict(groups=B_KV_HEADS, rep=B_HEADS // B_KV_HEADS, dq=B_HEAD_DIM, dv=B_HEAD_DIM)
            conv_params = None
        else:
            w = od_w_in[i]
            w_in = jnp.concatenate([w[:, :OD_KV_END], w[:, OD_KV_END:][:, perm_d],
                                    jnp.zeros((d, 128 - MLA_ROPE), F32)], axis=1).astype(BF16)
            wuq = od_w_uq[i].reshape(Q_LORA, MLA_HEADS, MLA_NOPE + MLA_ROPE)
            w_uq = jnp.concatenate([wuq[:, :, :MLA_NOPE], wuq[:, :, MLA_NOPE:][:, :, perm_d],
                                    jnp.zeros((Q_LORA, MLA_HEADS, MLA_QK_PAD - MLA_NOPE - MLA_ROPE), F32)],
                                   axis=-1).reshape(Q_LORA, -1).astype(BF16)
            wukv = od_w_ukv[i].reshape(KV_LORA, MLA_HEADS, MLA_NOPE + MLA_V)
            w_ukv = jnp.concatenate([wukv[:, :, :MLA_NOPE].reshape(KV_LORA, -1),
                                     wukv[:, :, MLA_NOPE:].reshape(KV_LORA, -1)], axis=1).astype(BF16)
            w_out = od_w_out[i].astype(BF16)
            pr = {s: _od_in(xs[s], mod[s], g_pre1, w_in, od_g_cq[i][None, :], od_g_ckv[i][None, :],
                            w_uq, w_ukv, od_cos[s], od_sa[s], od_sb[s], tms[s]) for s in streams}
            geom = dict(groups=MLA_HEADS, rep=1, dq=MLA_QK_PAD, dv=MLA_V)
            conv_params = (od_conv_w[i], od_conv_b[i][None, :], od_c_ln[i, 0][None, :], od_c_ln[i, 1][None, :])

        fw_in = ffn_w_in[layer].astype(BF16)
        fw_out = ffn_w_out[layer].astype(BF16)
        fcw, fcb = ffn_conv_w[layer], ffn_conv_b[layer][None, :]
        kv_of = {"lat": ("ctx", "lat"), "ctx": ("ctx",)}
        new_xs = {}
        for s in (streams if more else ("lat",)):
            b1, q = pr[s][0], pr[s][1]
            ks = [pr[t][2] for t in kv_of[s]]
            vs = [pr[t][3] for t in kv_of[s]]
            att = _attention(q, ks, vs, tq=min(tq, q.shape[1]), **geom)
            x1 = _proj(xs[s], mod[s], g_post1, b1, att, w_out, tms[s], conv_params)
            new_xs[s] = _ffn(x1, mod[s], g_pre2, g_post2, fw_in, fcw, fcb, fw_out, tms[s])
        xs = new_xs
    return xs["lat"]
```

```python
import functools

import jax
import jax.numpy as jnp
import numpy as np
from jax import lax
from jax.experimental import pallas as pl
from jax.experimental.pallas import tpu as pltpu

F32 = jnp.float32
BF16 = jnp.bfloat16

GRID_W = 64
ROPE_THETA = 10000.0
EPS = 1e-6

CHUNK = 128
A_GROUPS = 4
A_GROUP_DIM = 128
D_A = A_GROUPS * A_GROUP_DIM
B_HEADS = 8
B_KV_HEADS = 2
B_HEAD_DIM = 128
EV_A_END = 2 * D_A
EV_Q_END = EV_A_END + B_HEADS * B_HEAD_DIM
EV_K_END = EV_Q_END + B_KV_HEADS * B_HEAD_DIM
EV_IN = EV_K_END + B_KV_HEADS * B_HEAD_DIM

D_C = 512
CONV_W = 31
MLA_HEADS = 8
MLA_NOPE = 128
MLA_ROPE = 64
MLA_V = 128
MLA_QK_PAD = 256
Q_LORA = 512
KV_LORA = 256
OD_C_END = 2 * D_C
OD_Q_END = OD_C_END + Q_LORA
OD_KV_END = OD_Q_END + KV_LORA
OD_IN_PAD = OD_KV_END + 128

D_FF = 2816
FFN_CHUNK = 256

ATTN_KEY_CHUNK = 256
ATTN_Q_UNIT = 256
ATTN_SCORE_LEAD = 8
LOG2E = 1.4426950408889634

CONV_HALO = 16
FFN_HALO = 8

V7X_VMEM_LIMIT = 56 * 1024 * 1024


def _rms(xf, g):
    return xf * lax.rsqrt(jnp.mean(xf * xf, axis=-1, keepdims=True) + EPS) * g


def _layernorm(xf, g, b):
    mu = jnp.mean(xf, axis=-1, keepdims=True)
    xc = xf - mu
    return xc * lax.rsqrt(jnp.mean(xc * xc, axis=-1, keepdims=True) + EPS) * g + b


def _sigmoid(x):
    return 1.0 / (1.0 + jnp.exp(-x))


def _gelu_tanh(x):
    return 0.5 * x * (1.0 + jnp.tanh(np.float32(np.sqrt(2.0 / np.pi)) * (x + 0.044715 * (x * x * x))))


def _dot(a, b):
    return jnp.dot(a, b, preferred_element_type=F32)


def _params(sem):
    return pltpu.CompilerParams(dimension_semantics=sem, vmem_limit_bytes=V7X_VMEM_LIMIT)


def _const_spec(shape):
    nd = len(shape)
    return pl.BlockSpec(shape, lambda *_: (0,) * nd, pipeline_mode=pl.Buffered(1))


def _tok_spec(tm, width):
    return pl.BlockSpec((None, tm, width), lambda b, i: (b, i, 0))


def _mod_spec(mod):
    d = mod.shape[-1]
    if mod.shape[0] == 1:
        return pl.BlockSpec((None, 6, d), lambda b, i: (0, 0, 0))
    return pl.BlockSpec((None, 6, d), lambda b, i: (b, 0, 0))


def _halo_specs(tm, halo, n, width):
    r = tm // halo
    last = n // halo - 1
    prev = pl.BlockSpec((None, halo, width), lambda b, i: (b, jnp.maximum(i * r - 1, 0), 0))
    nxt = pl.BlockSpec((None, halo, width), lambda b, i: (b, jnp.minimum((i + 1) * r, last), 0))
    return prev, nxt


def _mod_kernel(s_ref, w_ref, b_ref, o_ref):
    s = s_ref[...]
    s = s * _sigmoid(s)
    o_ref[...] = jnp.dot(s, w_ref[...], preferred_element_type=F32,
                         precision=lax.Precision.HIGHEST) + b_ref[...]


def _modulation(cc, mod_w, mod_b):
    depth, d, d6 = mod_w.shape
    r = cc.shape[0]
    out = pl.pallas_call(
        _mod_kernel,
        grid=(depth, d6 // d),
        in_specs=[pl.BlockSpec((r, d), lambda l, j: (0, 0)),
                  pl.BlockSpec((None, d, d), lambda l, j: (l, 0, j)),
                  pl.BlockSpec((None, 1, d), lambda l, j: (l, 0, j))],
        out_specs=pl.BlockSpec((None, r, d), lambda l, j: (l, 0, j)),
        out_shape=jax.ShapeDtypeStruct((depth, r, d6), F32),
        compiler_params=_params(("arbitrary", "arbitrary")),
        name="modulation",
    )(cc, mod_w, mod_b.reshape(depth, 1, d6))
    return out.reshape(depth, r, 6, d)


def _ev_in_kernel(x_ref, mod_ref, g_ref, w_ref, lng_ref, lnb_ref, ws_ref, bs_ref, gq_ref, gk_ref,
                  cos_ref, sin_ref, a_ref, q_ref, k_ref, v_ref, *, tm):
    h = _rms(x_ref[...], g_ref[...]) * (1.0 + mod_ref[1:2, :]) + mod_ref[0:1, :]
    z = _dot(h.astype(BF16), w_ref[...])

    ga = _gelu_tanh(z[:, :EV_A_END])
    u = ga[:, :D_A]
    vv = _layernorm(ga[:, D_A:], lng_ref[...], lnb_ref[...]).astype(BF16)
    for c in range(tm // CHUNK):
        rows = slice(c * CHUNK, (c + 1) * CHUNK)
        for g in range(A_GROUPS):
            cols = slice(g * A_GROUP_DIM, (g + 1) * A_GROUP_DIM)
            s = _dot(ws_ref[g], vv[rows, cols]) + bs_ref[g]
            a_ref[rows, cols] = (u[rows, cols] * s).astype(a_ref.dtype)

    cosf = cos_ref[...]
    sinf = sin_ref[...]

    def head(zh, g):
        y = _rms(zh, g)
        return y * cosf + pltpu.roll(y, B_HEAD_DIM // 2, 1) * sinf

    scale = np.float32(B_HEAD_DIM ** -0.5 * LOG2E)
    for hd in range(B_HEADS):
        cols = slice(hd * B_HEAD_DIM, (hd + 1) * B_HEAD_DIM)
        zh = z[:, EV_A_END + hd * B_HEAD_DIM:EV_A_END + (hd + 1) * B_HEAD_DIM]
        q_ref[:, cols] = (head(zh, gq_ref[...]) * scale).astype(q_ref.dtype)
    for hd in range(B_KV_HEADS):
        cols = slice(hd * B_HEAD_DIM, (hd + 1) * B_HEAD_DIM)
        zh = z[:, EV_Q_END + hd * B_HEAD_DIM:EV_Q_END + (hd + 1) * B_HEAD_DIM]
        k_ref[:, cols] = head(zh, gk_ref[...]).astype(k_ref.dtype)
    v_ref[...] = z[:, EV_K_END:].astype(v_ref.dtype)


def _ev_in(x, mod, g_pre, w_in, ln_g, ln_b, w_s, b_s, gq, gk, cosf, sinf, tm):
    b, n, d = x.shape
    kvw = B_KV_HEADS * B_HEAD_DIM
    qw = B_HEADS * B_HEAD_DIM
    return pl.pallas_call(
        functools.partial(_ev_in_kernel, tm=tm),
        grid=(b, n // tm),
        in_specs=[_tok_spec(tm, d), _mod_spec(mod), _const_spec((1, d)), _const_spec(w_in.shape),
                  _const_spec((1, D_A)), _const_spec((1, D_A)), _const_spec(w_s.shape), _const_spec(b_s.shape),
                  _const_spec((1, B_HEAD_DIM)), _const_spec((1, B_HEAD_DIM)),
                  pl.BlockSpec((tm, B_HEAD_DIM), lambda bb, i: (i, 0)),
                  pl.BlockSpec((tm, B_HEAD_DIM), lambda bb, i: (i, 0))],
        out_specs=[_tok_spec(tm, D_A), _tok_spec(tm, qw), _tok_spec(tm, kvw), _tok_spec(tm, kvw)],
        out_shape=[jax.ShapeDtypeStruct((b, n, D_A), BF16), jax.ShapeDtypeStruct((b, n, qw), BF16),
                   jax.ShapeDtypeStruct((b, n, kvw), BF16), jax.ShapeDtypeStruct((b, n, kvw), BF16)],
        compiler_params=_params(("parallel", "parallel")),
        name="ev_in",
    )(x, mod, g_pre, w_in, ln_g, ln_b, w_s, b_s, gq, gk, cosf, sinf)


def _od_in_kernel(x_ref, mod_ref, g_ref, w_ref, gcq_ref, gckv_ref, wuq_ref, wukv_ref,
                  cos_ref, sa_ref, sb_ref, glu_ref, q_ref, k_ref, v_ref):
    h = _rms(x_ref[...], g_ref[...]) * (1.0 + mod_ref[1:2, :]) + mod_ref[0:1, :]
    z = _dot(h.astype(BF16), w_ref[...])

    glu_ref[...] = z[:, :D_C] * _sigmoid(z[:, D_C:OD_C_END])

    cosf = cos_ref[...]
    sa = sa_ref[...]
    sb = sb_ref[...]

    def rope(y):
        return y * cosf + pltpu.roll(y, MLA_ROPE // 2, 1) * sa + pltpu.roll(y, 128 - MLA_ROPE // 2, 1) * sb

    scale = np.float32((MLA_NOPE + MLA_ROPE) ** -0.5 * LOG2E)
    cq = _rms(z[:, OD_C_END:OD_Q_END], gcq_ref[...]).astype(BF16)
    qp = _dot(cq, wuq_ref[...])
    ckv = _rms(z[:, OD_Q_END:OD_KV_END], gckv_ref[...]).astype(BF16)
    kv = _dot(ckv, wukv_ref[...])
    kr = rope(z[:, OD_KV_END:OD_IN_PAD]).astype(k_ref.dtype)
    for hd in range(MLA_HEADS):
        lo = hd * MLA_QK_PAD
        mid = lo + MLA_NOPE
        hi = lo + MLA_QK_PAD
        q_ref[:, lo:mid] = (qp[:, lo:mid] * scale).astype(q_ref.dtype)
        q_ref[:, mid:hi] = (rope(qp[:, mid:hi]) * scale).astype(q_ref.dtype)
        k_ref[:, lo:mid] = kv[:, hd * MLA_NOPE:(hd + 1) * MLA_NOPE].astype(k_ref.dtype)
        k_ref[:, mid:hi] = kr
    v_ref[...] = kv[:, MLA_HEADS * MLA_NOPE:].astype(v_ref.dtype)


def _od_in(x, mod, g_pre, w_in, g_cq, g_ckv, w_uq, w_ukv, cosf, sa, sb, tm):
    b, n, d = x.shape
    qkw = MLA_HEADS * MLA_QK_PAD
    vw = MLA_HEADS * MLA_V
    tab = pl.BlockSpec((tm, 128), lambda bb, i: (i, 0))
    return pl.pallas_call(
        _od_in_kernel,
        grid=(b, n // tm),
        in_specs=[_tok_spec(tm, d), _mod_spec(mod), _const_spec((1, d)), _const_spec(w_in.shape),
                  _const_spec((1, Q_LORA)), _const_spec((1, KV_LORA)), _const_spec(w_uq.shape),
                  _const_spec(w_ukv.shape), tab, tab, tab],
        out_specs=[_tok_spec(tm, D_C), _tok_spec(tm, qkw), _tok_spec(tm, qkw), _tok_spec(tm, vw)],
        out_shape=[jax.ShapeDtypeStruct((b, n, D_C), F32), jax.ShapeDtypeStruct((b, n, qkw), BF16),
                   jax.ShapeDtypeStruct((b, n, qkw), BF16), jax.ShapeDtypeStruct((b, n, vw), BF16)],
        compiler_params=_params(("parallel", "parallel")),
        name="od_in",
    )(x, mod, g_pre, w_in, g_cq, g_ckv, w_uq, w_ukv, cosf, sa, sb)


def _attn_kernel(*refs, nseg, nh, nkv, dq, dv):
    q_ref = refs[0]
    k_refs = refs[1:1 + nseg]
    v_refs = refs[1 + nseg:1 + 2 * nseg]
    o_ref, s_ref, vt_ref = refs[1 + 2 * nseg:]
    tq = q_ref.shape[0]
    kc = ATTN_KEY_CHUNK
    seg_len = [k.shape[0] for k in k_refs]
    seg_off = [sum(seg_len[:i]) for i in range(nseg)]
    n_chunks = sum(seg_len) // kc

    @pl.when(pl.program_id(2) == 0)
    def _():
        for j in range(nkv):
            for v_ref, off, ln in zip(v_refs, seg_off, seg_len):
                for c in range(ln // kc):
                    blk = v_ref[c * kc:(c + 1) * kc, j * dv:(j + 1) * dv]
                    vt_ref[j, off // kc + c] = blk.astype(F32).T.astype(BF16)

    chunk_src = [(k_ref, c * kc) for k_ref, ln in zip(k_refs, seg_len) for c in range(ln // kc)]
    tu = ATTN_Q_UNIT
    units = [(h, r) for h in range(nh) for r in range(tq // tu)]

    assert dv <= kc and dq <= kc
    kpad = kc - dq

    def load_qt(u):
        h, r = units[u]
        qt = q_ref[r * tu:(r + 1) * tu, h * dq:(h + 1) * dq].astype(F32).T.astype(BF16)
        return qt if kpad == 0 else jnp.concatenate([qt, jnp.zeros((kpad, tu), BF16)], axis=0)

    def score_chunk(u, qt, c, m8):
        j = units[u][0] if nkv > 1 else 0
        k_ref, r0 = chunk_src[c]
        for r1 in range(0, kc, dv):
            kk = k_ref[r0 + r1:r0 + r1 + dv, j * dq:(j + 1) * dq]
            if kpad:
                kk = jnp.concatenate([kk, jnp.zeros((dv, kpad), BF16)], axis=1)
            s = _dot(kk, qt)
            s_ref[u % 3, c * kc + r1:c * kc + r1 + dv, :] = s
            s8 = jnp.max(s.reshape(dv // 8, 8, tu), axis=0)
            m8 = s8 if m8 is None else jnp.maximum(m8, s8)
        return m8

    n_steps = len(units) * n_chunks
    state = {"m8": None, "qt": None, "mx": {}}

    def score_step(t):
        u, c = divmod(t, n_chunks)
        if c == 0:
            state["qt"] = load_qt(u)
            state["m8"] = None
        state["m8"] = score_chunk(u, state["qt"], c, state["m8"])
        if c == n_chunks - 1:
            state["mx"][u] = jnp.max(state["m8"], axis=0, keepdims=True)

    lead = n_chunks + min(ATTN_SCORE_LEAD, n_chunks)
    for t in range(min(lead, n_steps)):
        score_step(t)
    l8 = o = None
    for t in range(n_steps):
        u, c = divmod(t, n_chunks)
        h, r = units[u]
        j = h if nkv > 1 else 0
        p = jnp.exp2(s_ref[u % 3, c * kc:(c + 1) * kc, :] - state["mx"][u])
        p8 = jnp.sum(p.reshape(kc // 8, 8, tu), axis=0)
        oc = _dot(vt_ref[j, c], p.astype(BF16))
        l8 = p8 if c == 0 else l8 + p8
        o = oc if c == 0 else o + oc
        if t + lead < n_steps:
            score_step(t + lead)
        if c == n_chunks - 1:
            o = o * (1.0 / jnp.sum(l8, axis=0, keepdims=True))
            o_ref[r * tu:(r + 1) * tu, h * dv:(h + 1) * dv] = o.T.astype(o_ref.dtype)


def _attention(q, ks, vs, nh, nkv, dq, dv, tq):
    b, n, qw = q.shape
    nseg = len(ks)
    groups = qw // (nh * dq)
    m_total = sum(k.shape[1] for k in ks)
    kv_specs = ([pl.BlockSpec((None, k.shape[1], nkv * dq), lambda bb, g, i: (bb, 0, g)) for k in ks]
                + [pl.BlockSpec((None, v.shape[1], nkv * dv), lambda bb, g, i: (bb, 0, g)) for v in vs])
    return pl.pallas_call(
        functools.partial(_attn_kernel, nseg=nseg, nh=nh, nkv=nkv, dq=dq, dv=dv),
        grid=(b, groups, n // tq),
        in_specs=[pl.BlockSpec((None, tq, nh * dq), lambda bb, g, i: (bb, i, g))] + kv_specs,
        out_specs=pl.BlockSpec((None, tq, nh * dv), lambda bb, g, i: (bb, i, g)),
        out_shape=jax.ShapeDtypeStruct((b, n, groups * nh * dv), BF16),
        scratch_shapes=[pltpu.VMEM((3, m_total, ATTN_Q_UNIT), F32),
                        pltpu.VMEM((nkv, m_total // ATTN_KEY_CHUNK, dv, ATTN_KEY_CHUNK), BF16)],
        compiler_params=_params(("parallel", "parallel", "arbitrary")),
        name="attention",
    )(q, *ks, *vs)


def _conformer_tail(main_ref, prev_ref, next_ref, cw_ref, cb_ref, lng_ref, lnb_ref, tm):
    i = pl.program_id(1)
    prev = jnp.where(i > 0, prev_ref[...], 0.0)
    nxt = jnp.where(i < pl.num_programs(1) - 1, next_ref[...], 0.0)
    ext = jnp.concatenate([prev, main_ref[...], nxt], axis=0)
    half = CONV_W // 2
    acc = None
    for r in range(8):
        rolled = ext if r == 0 else pltpu.roll(ext, r, 0)
        for a in range(-2, 2):
            tap = half - (8 * a + r)
            if 0 <= tap < CONV_W:
                lo = CONV_HALO - 8 * a
                term = cw_ref[tap:tap + 1, :] * rolled[lo:lo + tm]
                acc = term if acc is None else acc + term
    hc = _layernorm(acc + cb_ref[...], lng_ref[...], lnb_ref[...])
    return hc * _sigmoid(hc)


def _proj_kernel(*refs, conv, tm, w1):
    if conv:
        (x_ref, mod_ref, g_ref, b1_ref, b1p_ref, b1n_ref, att_ref, w_ref,
         cw_ref, cb_ref, lng_ref, lnb_ref, o_ref) = refs
        b1 = _conformer_tail(b1_ref, b1p_ref, b1n_ref, cw_ref, cb_ref, lng_ref, lnb_ref, tm).astype(BF16)
    else:
        x_ref, mod_ref, g_ref, b1_ref, att_ref, w_ref, o_ref = refs
        b1 = b1_ref[...]
    o = _dot(b1, w_ref[:w1, :]) + _dot(att_ref[...], w_ref[w1:, :])
    o_ref[...] = x_ref[...] + mod_ref[2:3, :] * _rms(o, g_ref[...])


def _proj(x, mod, g_post, b1, att, w_out, tm, conv_params=None):
    b, n, d = x.shape
    w1 = b1.shape[-1]
    conv = conv_params is not None
    in_specs = [_tok_spec(tm, d), _mod_spec(mod), _const_spec((1, d)), _tok_spec(tm, w1)]
    args = [x, mod, g_post, b1]
    if conv:
        in_specs += list(_halo_specs(tm, CONV_HALO, n, w1))
        args += [b1, b1]
    in_specs += [_tok_spec(tm, att.shape[-1]), _const_spec(w_out.shape)]
    args += [att, w_out]
    if conv:
        in_specs += [_const_spec(p.shape) for p in conv_params]
        args += list(conv_params)
    return pl.pallas_call(
        functools.partial(_proj_kernel, conv=conv, tm=tm, w1=w1),
        grid=(b, n // tm),
        in_specs=in_specs,
        out_specs=_tok_spec(tm, d),
        out_shape=jax.ShapeDtypeStruct((b, n, d), F32),
        compiler_params=_params(("parallel", "parallel")),
        name="proj_conv" if conv else "proj",
    )(*args)


def _ffn_kernel(x_ref, xp_ref, xn_ref, mod_ref, g2_ref, gp2_ref, win_ref, cw_ref, cb_ref, wout_ref,
                o_ref, acc_ref, *, tm):
    i = pl.program_id(1)
    shift = mod_ref[3:4, :]
    scale1 = 1.0 + mod_ref[4:5, :]

    def pre(xx):
        return _rms(xx, g2_ref[...]) * scale1 + shift

    x = x_ref[...]
    hm = pre(x)
    hp = jnp.where(i > 0, pre(xp_ref[...]), 0.0)
    hn = jnp.where(i < pl.num_programs(1) - 1, pre(xn_ref[...]), 0.0)
    hext = jnp.concatenate([hp, hm, hn], axis=0).astype(BF16)
    hmb = hm.astype(BF16)
    rows = tm + 2 * FFN_HALO
    for c in range(D_FF // FFN_CHUNK):
        cols = slice(c * FFN_CHUNK, (c + 1) * FFN_CHUNK)
        zg = _dot(hext, win_ref[:, cols])
        zu = _dot(hmb, win_ref[:, D_FF + c * FFN_CHUNK:D_FF + (c + 1) * FFN_CHUNK])
        g = (cw_ref[0:1, cols] * pltpu.roll(zg, 1, 0)[FFN_HALO:FFN_HALO + tm]
             + cw_ref[1:2, cols] * zg[FFN_HALO:FFN_HALO + tm]
             + cw_ref[2:3, cols] * pltpu.roll(zg, rows - 1, 0)[FFN_HALO:FFN_HALO + tm]
             + cb_ref[:, cols])
        act = (g * _sigmoid(g) * zu).astype(BF16)
        contrib = _dot(act, wout_ref[cols, :])
        if c == 0:
            acc_ref[...] = contrib
        else:
            acc_ref[...] += contrib
    o_ref[...] = x + mod_ref[5:6, :] * _rms(acc_ref[...], gp2_ref[...])


def _ffn(x, mod, g_pre2, g_post2, w_in, conv_w, conv_b, w_out, tm):
    b, n, d = x.shape
    prev, nxt = _halo_specs(tm, FFN_HALO, n, d)
    return pl.pallas_call(
        functools.partial(_ffn_kernel, tm=tm),
        grid=(b, n // tm),
        in_specs=[_tok_spec(tm, d), prev, nxt, _mod_spec(mod), _const_spec((1, d)), _const_spec((1, d)),
                  _const_spec(w_in.shape), _const_spec(conv_w.shape), _const_spec(conv_b.shape),
                  _const_spec(w_out.shape)],
        out_specs=_tok_spec(tm, d),
        out_shape=jax.ShapeDtypeStruct((b, n, d), F32),
        scratch_shapes=[pltpu.VMEM((tm, d), F32)],
        compiler_params=_params(("parallel", "parallel")),
        name="conv_ffn",
    )(x, x, x, mod, g_pre2, g_post2, w_in, conv_w, conv_b, w_out)


def _deinterleave(d):
    return np.concatenate([np.arange(0, d, 2), np.arange(1, d, 2)])


def _rope_angles(n, d_rot):
    t = jnp.arange(n)
    row = (t // GRID_W).astype(F32)
    col = (t % GRID_W).astype(F32)
    n_freq = d_rot // 4
    freq = ROPE_THETA ** (-jnp.arange(n_freq, dtype=F32) / n_freq)
    ang = jnp.concatenate([row[:, None] * freq, col[:, None] * freq], axis=-1)
    return jnp.cos(ang), jnp.sin(ang)


def kernel(x, c, ctx, c_ctx, mod_w, mod_b, norm_g, ffn_w_in, ffn_conv_w, ffn_conv_b, ffn_w_out,
           ev_w_in, ev_a_ln, ev_w_s, ev_b_s, ev_qk_g, ev_w_out,
           od_w_in, od_conv_w, od_conv_b, od_c_ln, od_g_cq, od_g_ckv, od_w_uq, od_w_ukv, od_w_out):
    b, n, d = x.shape
    l_ctx = ctx.shape[1]
    depth = mod_w.shape[0]
    tm_lat, tm_ctx, tq = 512, l_ctx, 1024

    n_rows = -(-(b + 1) // 8) * 8
    cc = jnp.concatenate([c, c_ctx[None, :], jnp.zeros((n_rows - b - 1, d), F32)], axis=0)
    mods = _modulation(cc, mod_w, mod_b)

    cos_b, sin_b = _rope_angles(n, B_HEAD_DIM)
    ev_cos = {"lat": jnp.concatenate([cos_b, cos_b], axis=-1), "ctx": jnp.ones((l_ctx, B_HEAD_DIM), F32)}
    ev_sin = {"lat": jnp.concatenate([-sin_b, sin_b], axis=-1), "ctx": jnp.zeros((l_ctx, B_HEAD_DIM), F32)}
    cos_d, sin_d = _rope_angles(n, MLA_ROPE)
    hr = MLA_ROPE // 2
    zpad = jnp.zeros((n, 128 - MLA_ROPE), F32)
    od_cos = {"lat": jnp.concatenate([cos_d, cos_d, zpad], axis=-1),
              "ctx": jnp.concatenate([jnp.ones((l_ctx, MLA_ROPE), F32), jnp.zeros((l_ctx, 128 - MLA_ROPE), F32)], -1)}
    od_sa = {"lat": jnp.concatenate([jnp.zeros((n, hr), F32), sin_d, zpad], axis=-1),
             "ctx": jnp.zeros((l_ctx, 128), F32)}
    od_sb = {"lat": jnp.concatenate([-sin_d, jnp.zeros((n, hr), F32), zpad], axis=-1),
             "ctx": jnp.zeros((l_ctx, 128), F32)}

    perm_b = _deinterleave(B_HEAD_DIM)
    perm_d = _deinterleave(MLA_ROPE)

    xs = {"lat": x, "ctx": ctx}
    tms = {"lat": tm_lat, "ctx": tm_ctx}
    for layer in range(depth):
        more = layer < depth - 1
        i = layer // 2
        mod = {"lat": mods[layer, :b], "ctx": mods[layer, b:b + 1]}
        g_pre1, g_post1, g_pre2, g_post2 = (norm_g[layer, j][None, :] for j in range(4))
        streams = ("lat", "ctx")
        if layer % 2 == 0:
            w = ev_w_in[i]
            wq = w[:, EV_A_END:EV_Q_END].reshape(d, B_HEADS, B_HEAD_DIM)[:, :, perm_b].reshape(d, -1)
            wk = w[:, EV_Q_END:EV_K_END].reshape(d, B_KV_HEADS, B_HEAD_DIM)[:, :, perm_b].reshape(d, -1)
            w_in = jnp.concatenate([w[:, :EV_A_END], wq, wk, w[:, EV_K_END:]], axis=1).astype(BF16)
            ln_g, ln_b = ev_a_ln[i, 0][None, :], ev_a_ln[i, 1][None, :]
            w_s = ev_w_s[i].astype(BF16)
            b_s = jnp.broadcast_to(ev_b_s[i][:, :, None], (A_GROUPS, CHUNK, A_GROUP_DIM))
            gq, gk = ev_qk_g[i, 0][perm_b][None, :], ev_qk_g[i, 1][perm_b][None, :]
            w_out = ev_w_out[i].astype(BF16)
            pr = {s: _ev_in(xs[s], mod[s], g_pre1, w_in, ln_g, ln_b, w_s, b_s, gq, gk,
                            ev_cos[s], ev_sin[s], tms[s]) for s in streams}
            geom = dict(nh=B_HEADS // B_KV_HEADS, nkv=1, dq=B_HEAD_DIM, dv=B_HEAD_DIM)
            conv_params = None
        else:
            w = od_w_in[i]
            w_in = jnp.concatenate([w[:, :OD_KV_END], w[:, OD_KV_END:][:, perm_d],
                                    jnp.zeros((d, 128 - MLA_ROPE), F32)], axis=1).astype(BF16)
            wuq = od_w_uq[i].reshape(Q_LORA, MLA_HEADS, MLA_NOPE + MLA_ROPE)
            w_uq = jnp.concatenate([wuq[:, :, :MLA_NOPE], wuq[:, :, MLA_NOPE:][:, :, perm_d],
                                    jnp.zeros((Q_LORA, MLA_HEADS, MLA_QK_PAD - MLA_NOPE - MLA_ROPE), F32)],
                                   axis=-1).reshape(Q_LORA, -1).astype(BF16)
            wukv = od_w_ukv[i].reshape(KV_LORA, MLA_HEADS, MLA_NOPE + MLA_V)
            w_ukv = jnp.concatenate([wukv[:, :, :MLA_NOPE].reshape(KV_LORA, -1),
                                     wukv[:, :, MLA_NOPE:].reshape(KV_LORA, -1)], axis=1).astype(BF16)
            w_out = od_w_out[i].astype(BF16)
            pr = {s: _od_in(xs[s], mod[s], g_pre1, w_in, od_g_cq[i][None, :], od_g_ckv[i][None, :],
                            w_uq, w_ukv, od_cos[s], od_sa[s], od_sb[s], tms[s]) for s in streams}
            geom = dict(nh=2, nkv=2, dq=MLA_QK_PAD, dv=MLA_V)
            conv_params = (od_conv_w[i], od_conv_b[i][None, :], od_c_ln[i, 0][None, :], od_c_ln[i, 1][None, :])

        fw_in = ffn_w_in[layer].astype(BF16)
        fw_out = ffn_w_out[layer].astype(BF16)
        fcw, fcb = ffn_conv_w[layer], ffn_conv_b[layer][None, :]
        kv_of = {"lat": ("ctx", "lat"), "ctx": ("ctx",)}
        new_xs = {}
        for s in (streams if more else ("lat",)):
            b1, q = pr[s][0], pr[s][1]
            ks = [pr[t][2] for t in kv_of[s]]
            vs = [pr[t][3] for t in kv_of[s]]
            att = _attention(q, ks, vs, tq=min(tq, q.shape[1]), **geom)
            x1 = _proj(xs[s], mod[s], g_post1, b1, att, w_out, tms[s], conv_params)
            new_xs[s] = _ffn(x1, mod[s], g_pre2, g_post2, fw_in, fcw, fcb, fw_out, tms[s])
        xs = new_xs
    return xs["lat"]
```

```python
import functools

import jax
import jax.numpy as jnp
import numpy as np
from jax import lax
from jax.experimental import pallas as pl
from jax.experimental.pallas import tpu as pltpu

F32 = jnp.float32
BF16 = jnp.bfloat16

GRID_W = 64
ROPE_THETA = 10000.0
EPS = 1e-6

CHUNK = 128
A_GROUPS = 4
A_GROUP_DIM = 128
D_A = A_GROUPS * A_GROUP_DIM
B_HEADS = 8
B_KV_HEADS = 2
B_HEAD_DIM = 128
EV_A_END = 2 * D_A
EV_Q_END = EV_A_END + B_HEADS * B_HEAD_DIM
EV_K_END = EV_Q_END + B_KV_HEADS * B_HEAD_DIM
EV_IN = EV_K_END + B_KV_HEADS * B_HEAD_DIM

D_C = 512
CONV_W = 31
MLA_HEADS = 8
MLA_NOPE = 128
MLA_ROPE = 64
MLA_V = 128
MLA_QK_PAD = 256
Q_LORA = 512
KV_LORA = 256
OD_C_END = 2 * D_C
OD_Q_END = OD_C_END + Q_LORA
OD_KV_END = OD_Q_END + KV_LORA
OD_IN_PAD = OD_KV_END + 128

D_FF = 2816
FFN_CHUNK = 256

ATTN_KEY_CHUNK = 256
ATTN_Q_UNIT = 256
ATTN_SCORE_LEAD = 8
LOG2E = 1.4426950408889634

CONV_HALO = 16
FFN_HALO = 8
IN_SUBTILE = 256

V7X_VMEM_LIMIT = 56 * 1024 * 1024


def _rms(xf, g):
    return xf * lax.rsqrt(jnp.mean(xf * xf, axis=-1, keepdims=True) + EPS) * g


def _layernorm(xf, g, b):
    mu = jnp.mean(xf, axis=-1, keepdims=True)
    xc = xf - mu
    return xc * lax.rsqrt(jnp.mean(xc * xc, axis=-1, keepdims=True) + EPS) * g + b


def _sigmoid(x):
    return 1.0 / (1.0 + jnp.exp(-x))


def _gelu_tanh(x):
    return 0.5 * x * (1.0 + jnp.tanh(np.float32(np.sqrt(2.0 / np.pi)) * (x + 0.044715 * (x * x * x))))


def _dot(a, b):
    return jnp.dot(a, b, preferred_element_type=F32)


def _params(sem):
    return pltpu.CompilerParams(dimension_semantics=sem, vmem_limit_bytes=V7X_VMEM_LIMIT)


def _const_spec(shape):
    nd = len(shape)
    return pl.BlockSpec(shape, lambda *_: (0,) * nd, pipeline_mode=pl.Buffered(1))


def _tok_spec(tm, width):
    return pl.BlockSpec((None, tm, width), lambda b, i: (b, i, 0))


def _mod_spec(mod):
    d = mod.shape[-1]
    if mod.shape[0] == 1:
        return pl.BlockSpec((None, 6, d), lambda b, i: (0, 0, 0))
    return pl.BlockSpec((None, 6, d), lambda b, i: (b, 0, 0))


def _halo_specs(tm, halo, n, width):
    r = tm // halo
    last = n // halo - 1
    prev = pl.BlockSpec((None, halo, width), lambda b, i: (b, jnp.maximum(i * r - 1, 0), 0))
    nxt = pl.BlockSpec((None, halo, width), lambda b, i: (b, jnp.minimum((i + 1) * r, last), 0))
    return prev, nxt


def _mod_kernel(s_ref, w_ref, b_ref, o_ref):
    s = s_ref[...]
    s = s * _sigmoid(s)
    o_ref[...] = jnp.dot(s, w_ref[...], preferred_element_type=F32,
                         precision=lax.Precision.HIGHEST) + b_ref[...]


def _modulation(cc, mod_w, mod_b):
    depth, d, d6 = mod_w.shape
    r = cc.shape[0]
    out = pl.pallas_call(
        _mod_kernel,
        grid=(depth, d6 // d),
        in_specs=[pl.BlockSpec((r, d), lambda l, j: (0, 0)),
                  pl.BlockSpec((None, d, d), lambda l, j: (l, 0, j)),
                  pl.BlockSpec((None, 1, d), lambda l, j: (l, 0, j))],
        out_specs=pl.BlockSpec((None, r, d), lambda l, j: (l, 0, j)),
        out_shape=jax.ShapeDtypeStruct((depth, r, d6), F32),
        compiler_params=_params(("arbitrary", "arbitrary")),
        name="modulation",
    )(cc, mod_w, mod_b.reshape(depth, 1, d6))
    return out.reshape(depth, r, 6, d)


def _ev_in_kernel(x_ref, mod_ref, g_ref, w_ref, lng_ref, lnb_ref, ws_ref, bs_ref, gq_ref, gk_ref,
                  cos_ref, sin_ref, a_ref, q_ref, k_ref, v_ref, *, tm):
    scale = np.float32(B_HEAD_DIM ** -0.5 * LOG2E)
    sub = min(tm, IN_SUBTILE)
    for r0 in range(0, tm, sub):
        rs = slice(r0, r0 + sub)
        h = _rms(x_ref[rs, :], g_ref[...]) * (1.0 + mod_ref[1:2, :]) + mod_ref[0:1, :]
        z = _dot(h.astype(BF16), w_ref[...])

        ga = _gelu_tanh(z[:, :EV_A_END])
        u = ga[:, :D_A]
        vv = _layernorm(ga[:, D_A:], lng_ref[...], lnb_ref[...]).astype(BF16)
        for c in range(sub // CHUNK):
            rows = slice(c * CHUNK, (c + 1) * CHUNK)
            for g in range(A_GROUPS):
                cols = slice(g * A_GROUP_DIM, (g + 1) * A_GROUP_DIM)
                s = _dot(ws_ref[g], vv[rows, cols]) + bs_ref[g]
                a_ref[r0 + c * CHUNK:r0 + (c + 1) * CHUNK, cols] = (u[rows, cols] * s).astype(a_ref.dtype)

        cosf = cos_ref[rs, :]
        sinf = sin_ref[rs, :]

        def head(zh, g, cosf=cosf, sinf=sinf):
            y = _rms(zh, g)
            return y * cosf + pltpu.roll(y, B_HEAD_DIM // 2, 1) * sinf

        for hd in range(B_HEADS):
            cols = slice(hd * B_HEAD_DIM, (hd + 1) * B_HEAD_DIM)
            zh = z[:, EV_A_END + hd * B_HEAD_DIM:EV_A_END + (hd + 1) * B_HEAD_DIM]
            q_ref[rs, cols] = (head(zh, gq_ref[...]) * scale).astype(q_ref.dtype)
        for hd in range(B_KV_HEADS):
            cols = slice(hd * B_HEAD_DIM, (hd + 1) * B_HEAD_DIM)
            zh = z[:, EV_Q_END + hd * B_HEAD_DIM:EV_Q_END + (hd + 1) * B_HEAD_DIM]
            k_ref[rs, cols] = head(zh, gk_ref[...]).astype(k_ref.dtype)
        v_ref[rs, :] = z[:, EV_K_END:].astype(v_ref.dtype)


def _ev_in(x, mod, g_pre, w_in, ln_g, ln_b, w_s, b_s, gq, gk, cosf, sinf, tm):
    b, n, d = x.shape
    kvw = B_KV_HEADS * B_HEAD_DIM
    qw = B_HEADS * B_HEAD_DIM
    return pl.pallas_call(
        functools.partial(_ev_in_kernel, tm=tm),
        grid=(b, n // tm),
        in_specs=[_tok_spec(tm, d), _mod_spec(mod), _const_spec((1, d)), _const_spec(w_in.shape),
                  _const_spec((1, D_A)), _const_spec((1, D_A)), _const_spec(w_s.shape), _const_spec(b_s.shape),
                  _const_spec((1, B_HEAD_DIM)), _const_spec((1, B_HEAD_DIM)),
                  pl.BlockSpec((tm, B_HEAD_DIM), lambda bb, i: (i, 0)),
                  pl.BlockSpec((tm, B_HEAD_DIM), lambda bb, i: (i, 0))],
        out_specs=[_tok_spec(tm, D_A), _tok_spec(tm, qw), _tok_spec(tm, kvw), _tok_spec(tm, kvw)],
        out_shape=[jax.ShapeDtypeStruct((b, n, D_A), BF16), jax.ShapeDtypeStruct((b, n, qw), BF16),
                   jax.ShapeDtypeStruct((b, n, kvw), BF16), jax.ShapeDtypeStruct((b, n, kvw), BF16)],
        compiler_params=_params(("parallel", "parallel")),
        name="ev_in",
    )(x, mod, g_pre, w_in, ln_g, ln_b, w_s, b_s, gq, gk, cosf, sinf)


def _od_in_kernel(x_ref, mod_ref, g_ref, w_ref, gcq_ref, gckv_ref, wuq_ref, wukv_ref,
                  cos_ref, sa_ref, sb_ref, glu_ref, q_ref, k_ref, v_ref, *, tm):
    scale = np.float32((MLA_NOPE + MLA_ROPE) ** -0.5 * LOG2E)
    sub = min(tm, IN_SUBTILE)
    for r0 in range(0, tm, sub):
        rs = slice(r0, r0 + sub)
        h = _rms(x_ref[rs, :], g_ref[...]) * (1.0 + mod_ref[1:2, :]) + mod_ref[0:1, :]
        z = _dot(h.astype(BF16), w_ref[...])

        glu_ref[rs, :] = z[:, :D_C] * _sigmoid(z[:, D_C:OD_C_END])

        cosf = cos_ref[rs, :]
        sa = sa_ref[rs, :]
        sb = sb_ref[rs, :]

        def rope(y, cosf=cosf, sa=sa, sb=sb):
            return y * cosf + pltpu.roll(y, MLA_ROPE // 2, 1) * sa + pltpu.roll(y, 128 - MLA_ROPE // 2, 1) * sb

        cq = _rms(z[:, OD_C_END:OD_Q_END], gcq_ref[...]).astype(BF16)
        qp = _dot(cq, wuq_ref[...])
        ckv = _rms(z[:, OD_Q_END:OD_KV_END], gckv_ref[...]).astype(BF16)
        kv = _dot(ckv, wukv_ref[...])
        kr = rope(z[:, OD_KV_END:OD_IN_PAD]).astype(k_ref.dtype)
        for hd in range(MLA_HEADS):
            lo = hd * MLA_QK_PAD
            mid = lo + MLA_NOPE
            hi = lo + MLA_QK_PAD
            q_ref[rs, lo:mid] = (qp[:, lo:mid] * scale).astype(q_ref.dtype)
            q_ref[rs, mid:hi] = (rope(qp[:, mid:hi]) * scale).astype(q_ref.dtype)
            k_ref[rs, lo:mid] = kv[:, hd * MLA_NOPE:(hd + 1) * MLA_NOPE].astype(k_ref.dtype)
            k_ref[rs, mid:hi] = kr
        v_ref[rs, :] = kv[:, MLA_HEADS * MLA_NOPE:].astype(v_ref.dtype)


def _od_in(x, mod, g_pre, w_in, g_cq, g_ckv, w_uq, w_ukv, cosf, sa, sb, tm):
    b, n, d = x.shape
    qkw = MLA_HEADS * MLA_QK_PAD
    vw = MLA_HEADS * MLA_V
    tab = pl.BlockSpec((tm, 128), lambda bb, i: (i, 0))
    return pl.pallas_call(
        functools.partial(_od_in_kernel, tm=tm),
        grid=(b, n // tm),
        in_specs=[_tok_spec(tm, d), _mod_spec(mod), _const_spec((1, d)), _const_spec(w_in.shape),
                  _const_spec((1, Q_LORA)), _const_spec((1, KV_LORA)), _const_spec(w_uq.shape),
                  _const_spec(w_ukv.shape), tab, tab, tab],
        out_specs=[_tok_spec(tm, D_C), _tok_spec(tm, qkw), _tok_spec(tm, qkw), _tok_spec(tm, vw)],
        out_shape=[jax.ShapeDtypeStruct((b, n, D_C), F32), jax.ShapeDtypeStruct((b, n, qkw), BF16),
                   jax.ShapeDtypeStruct((b, n, qkw), BF16), jax.ShapeDtypeStruct((b, n, vw), BF16)],
        compiler_params=_params(("parallel", "parallel")),
        name="od_in",
    )(x, mod, g_pre, w_in, g_cq, g_ckv, w_uq, w_ukv, cosf, sa, sb)


def _attn_kernel(*refs, nseg, nh, nkv, dq, dv):
    q_ref = refs[0]
    k_refs = refs[1:1 + nseg]
    v_refs = refs[1 + nseg:1 + 2 * nseg]
    o_ref, s_ref, vt_ref = refs[1 + 2 * nseg:]
    tq = q_ref.shape[0]
    kc = ATTN_KEY_CHUNK
    seg_len = [k.shape[0] for k in k_refs]
    seg_off = [sum(seg_len[:i]) for i in range(nseg)]
    n_chunks = sum(seg_len) // kc

    @pl.when(pl.program_id(2) == 0)
    def _():
        for j in range(nkv):
            for v_ref, off, ln in zip(v_refs, seg_off, seg_len):
                for c in range(ln // kc):
                    blk = v_ref[c * kc:(c + 1) * kc, j * dv:(j + 1) * dv]
                    vt_ref[j, off // kc + c] = blk.astype(F32).T.astype(BF16)

    chunk_src = [(k_ref, c * kc) for k_ref, ln in zip(k_refs, seg_len) for c in range(ln // kc)]
    tu = ATTN_Q_UNIT
    units = [(h, r) for h in range(nh) for r in range(tq // tu)]

    assert dv <= kc and dq <= kc
    kpad = kc - dq

    def load_qt(u):
        h, r = units[u]
        qt = q_ref[r * tu:(r + 1) * tu, h * dq:(h + 1) * dq].astype(F32).T.astype(BF16)
        return qt if kpad == 0 else jnp.concatenate([qt, jnp.zeros((kpad, tu), BF16)], axis=0)

    def score_chunk(u, qt, c, m8):
        j = units[u][0] if nkv > 1 else 0
        k_ref, r0 = chunk_src[c]
        for r1 in range(0, kc, dv):
            kk = k_ref[r0 + r1:r0 + r1 + dv, j * dq:(j + 1) * dq]
            if kpad:
                kk = jnp.concatenate([kk, jnp.zeros((dv, kpad), BF16)], axis=1)
            s = _dot(kk, qt)
            s_ref[u % 3, c * kc + r1:c * kc + r1 + dv, :] = s
            s8 = jnp.max(s.reshape(dv // 8, 8, tu), axis=0)
            m8 = s8 if m8 is None else jnp.maximum(m8, s8)
        return m8

    n_steps = len(units) * n_chunks
    state = {"m8": None, "qt": None, "mx": {}}

    def score_step(t):
        u, c = divmod(t, n_chunks)
        if c == 0:
            state["qt"] = load_qt(u)
            state["m8"] = None
        state["m8"] = score_chunk(u, state["qt"], c, state["m8"])
        if c == n_chunks - 1:
            state["mx"][u] = jnp.max(state["m8"], axis=0, keepdims=True)

    lead = n_chunks + min(ATTN_SCORE_LEAD, n_chunks)
    for t in range(min(lead, n_steps)):
        score_step(t)
    l8 = o = None
    for t in range(n_steps):
        u, c = divmod(t, n_chunks)
        h, r = units[u]
        j = h if nkv > 1 else 0
        p = jnp.exp2(s_ref[u % 3, c * kc:(c + 1) * kc, :] - state["mx"][u])
        p8 = jnp.sum(p.reshape(kc // 8, 8, tu), axis=0)
        oc = _dot(vt_ref[j, c], p.astype(BF16))
        l8 = p8 if c == 0 else l8 + p8
        o = oc if c == 0 else o + oc
        if t + lead < n_steps:
            score_step(t + lead)
        if c == n_chunks - 1:
            o = o * (1.0 / jnp.sum(l8, axis=0, keepdims=True))
            o_ref[r * tu:(r + 1) * tu, h * dv:(h + 1) * dv] = o.T.astype(o_ref.dtype)


def _attention(q, ks, vs, nh, nkv, dq, dv, tq):
    b, n, qw = q.shape
    nseg = len(ks)
    groups = qw // (nh * dq)
    m_total = sum(k.shape[1] for k in ks)
    kv_specs = ([pl.BlockSpec((None, k.shape[1], nkv * dq), lambda bb, g, i: (bb, 0, g)) for k in ks]
                + [pl.BlockSpec((None, v.shape[1], nkv * dv), lambda bb, g, i: (bb, 0, g)) for v in vs])
    return pl.pallas_call(
        functools.partial(_attn_kernel, nseg=nseg, nh=nh, nkv=nkv, dq=dq, dv=dv),
        grid=(b, groups, n // tq),
        in_specs=[pl.BlockSpec((None, tq, nh * dq), lambda bb, g, i: (bb, i, g))] + kv_specs,
        out_specs=pl.BlockSpec((None, tq, nh * dv), lambda bb, g, i: (bb, i, g)),
        out_shape=jax.ShapeDtypeStruct((b, n, groups * nh * dv), BF16),
        scratch_shapes=[pltpu.VMEM((3, m_total, ATTN_Q_UNIT), F32),
                        pltpu.VMEM((nkv, m_total // ATTN_KEY_CHUNK, dv, ATTN_KEY_CHUNK), BF16)],
        compiler_params=_params(("parallel", "parallel", "arbitrary")),
        name="attention",
    )(q, *ks, *vs)


def _conformer_tail(main_ref, prev_ref, next_ref, cw_ref, cb_ref, lng_ref, lnb_ref, tm):
    i = pl.program_id(1)
    prev = jnp.where(i > 0, prev_ref[...], 0.0)
    nxt = jnp.where(i < pl.num_programs(1) - 1, next_ref[...], 0.0)
    ext = jnp.concatenate([prev, main_ref[...], nxt], axis=0)
    half = CONV_W // 2
    acc = None
    for r in range(8):
        rolled = ext if r == 0 else pltpu.roll(ext, r, 0)
        for a in range(-2, 2):
            tap = half - (8 * a + r)
            if 0 <= tap < CONV_W:
                lo = CONV_HALO - 8 * a
                term = cw_ref[tap:tap + 1, :] * rolled[lo:lo + tm]
                acc = term if acc is None else acc + term
    hc = _layernorm(acc + cb_ref[...], lng_ref[...], lnb_ref[...])
    return hc * _sigmoid(hc)


def _proj_kernel(*refs, conv, tm, w1):
    if conv:
        (x_ref, mod_ref, g_ref, b1_ref, b1p_ref, b1n_ref, att_ref, w_ref,
         cw_ref, cb_ref, lng_ref, lnb_ref, o_ref) = refs
        b1 = _conformer_tail(b1_ref, b1p_ref, b1n_ref, cw_ref, cb_ref, lng_ref, lnb_ref, tm).astype(BF16)
    else:
        x_ref, mod_ref, g_ref, b1_ref, att_ref, w_ref, o_ref = refs
        b1 = b1_ref[...]
    o = _dot(b1, w_ref[:w1, :]) + _dot(att_ref[...], w_ref[w1:, :])
    o_ref[...] = x_ref[...] + mod_ref[2:3, :] * _rms(o, g_ref[...])


def _proj(x, mod, g_post, b1, att, w_out, tm, conv_params=None):
    b, n, d = x.shape
    w1 = b1.shape[-1]
    conv = conv_params is not None
    in_specs = [_tok_spec(tm, d), _mod_spec(mod), _const_spec((1, d)), _tok_spec(tm, w1)]
    args = [x, mod, g_post, b1]
    if conv:
        in_specs += list(_halo_specs(tm, CONV_HALO, n, w1))
        args += [b1, b1]
    in_specs += [_tok_spec(tm, att.shape[-1]), _const_spec(w_out.shape)]
    args += [att, w_out]
    if conv:
        in_specs += [_const_spec(p.shape) for p in conv_params]
        args += list(conv_params)
    return pl.pallas_call(
        functools.partial(_proj_kernel, conv=conv, tm=tm, w1=w1),
        grid=(b, n // tm),
        in_specs=in_specs,
        out_specs=_tok_spec(tm, d),
        out_shape=jax.ShapeDtypeStruct((b, n, d), F32),
        compiler_params=_params(("parallel", "parallel")),
        name="proj_conv" if conv else "proj",
    )(*args)


def _ffn_kernel(x_ref, xp_ref, xn_ref, mod_ref, g2_ref, gp2_ref, win_ref, cw_ref, cb_ref, wout_ref,
                o_ref, act_ref, *, tm):
    i = pl.program_id(1)
    shift = mod_ref[3:4, :]
    scale1 = 1.0 + mod_ref[4:5, :]

    def pre(xx):
        return _rms(xx, g2_ref[...]) * scale1 + shift

    x = x_ref[...]
    hm = pre(x)
    hp = jnp.where(i > 0, pre(xp_ref[...]), 0.0)
    hn = jnp.where(i < pl.num_programs(1) - 1, pre(xn_ref[...]), 0.0)
    hext = jnp.concatenate([hp, hm, hn], axis=0).astype(BF16)
    hmb = hm.astype(BF16)
    rows = tm + 2 * FFN_HALO
    for c in range(D_FF // FFN_CHUNK):
        cols = slice(c * FFN_CHUNK, (c + 1) * FFN_CHUNK)
        zg = _dot(hext, win_ref[:, cols])
        zu = _dot(hmb, win_ref[:, D_FF + c * FFN_CHUNK:D_FF + (c + 1) * FFN_CHUNK])
        g = (cw_ref[0:1, cols] * pltpu.roll(zg, 1, 0)[FFN_HALO:FFN_HALO + tm]
             + cw_ref[1:2, cols] * zg[FFN_HALO:FFN_HALO + tm]
             + cw_ref[2:3, cols] * pltpu.roll(zg, rows - 1, 0)[FFN_HALO:FFN_HALO + tm]
             + cb_ref[:, cols])
        act_ref[:, cols] = (g * _sigmoid(g) * zu).astype(BF16)
    f = _dot(act_ref[...], wout_ref[...])
    o_ref[...] = x + mod_ref[5:6, :] * _rms(f, gp2_ref[...])


def _ffn(x, mod, g_pre2, g_post2, w_in, conv_w, conv_b, w_out, tm):
    b, n, d = x.shape
    prev, nxt = _halo_specs(tm, FFN_HALO, n, d)
    return pl.pallas_call(
        functools.partial(_ffn_kernel, tm=tm),
        grid=(b, n // tm),
        in_specs=[_tok_spec(tm, d), prev, nxt, _mod_spec(mod), _const_spec((1, d)), _const_spec((1, d)),
                  _const_spec(w_in.shape), _const_spec(conv_w.shape), _const_spec(conv_b.shape),
                  _const_spec(w_out.shape)],
        out_specs=_tok_spec(tm, d),
        out_shape=jax.ShapeDtypeStruct((b, n, d), F32),
        scratch_shapes=[pltpu.VMEM((tm, D_FF), BF16)],
        compiler_params=_params(("parallel", "parallel")),
        name="conv_ffn",
    )(x, x, x, mod, g_pre2, g_post2, w_in, conv_w, conv_b, w_out)


def _deinterleave(d):
    return np.concatenate([np.arange(0, d, 2), np.arange(1, d, 2)])


def _rope_angles(n, d_rot):
    t = jnp.arange(n)
    row = (t // GRID_W).astype(F32)
    col = (t % GRID_W).astype(F32)
    n_freq = d_rot // 4
    freq = ROPE_THETA ** (-jnp.arange(n_freq, dtype=F32) / n_freq)
    ang = jnp.concatenate([row[:, None] * freq, col[:, None] * freq], axis=-1)
    return jnp.cos(ang), jnp.sin(ang)


def kernel(x, c, ctx, c_ctx, mod_w, mod_b, norm_g, ffn_w_in, ffn_conv_w, ffn_conv_b, ffn_w_out,
           ev_w_in, ev_a_ln, ev_w_s, ev_b_s, ev_qk_g, ev_w_out,
           od_w_in, od_conv_w, od_conv_b, od_c_ln, od_g_cq, od_g_ckv, od_w_uq, od_w_ukv, od_w_out):
    b, n, d = x.shape
    l_ctx = ctx.shape[1]
    depth = mod_w.shape[0]
    tm_lat, tm_ctx = 512, l_ctx
    tm_in = {"lat": 1024, "ctx": l_ctx}

    n_rows = -(-(b + 1) // 8) * 8
    cc = jnp.concatenate([c, c_ctx[None, :], jnp.zeros((n_rows - b - 1, d), F32)], axis=0)
    mods = _modulation(cc, mod_w, mod_b)

    cos_b, sin_b = _rope_angles(n, B_HEAD_DIM)
    ev_cos = {"lat": jnp.concatenate([cos_b, cos_b], axis=-1), "ctx": jnp.ones((l_ctx, B_HEAD_DIM), F32)}
    ev_sin = {"lat": jnp.concatenate([-sin_b, sin_b], axis=-1), "ctx": jnp.zeros((l_ctx, B_HEAD_DIM), F32)}
    cos_d, sin_d = _rope_angles(n, MLA_ROPE)
    hr = MLA_ROPE // 2
    zpad = jnp.zeros((n, 128 - MLA_ROPE), F32)
    od_cos = {"lat": jnp.concatenate([cos_d, cos_d, zpad], axis=-1),
              "ctx": jnp.concatenate([jnp.ones((l_ctx, MLA_ROPE), F32), jnp.zeros((l_ctx, 128 - MLA_ROPE), F32)], -1)}
    od_sa = {"lat": jnp.concatenate([jnp.zeros((n, hr), F32), sin_d, zpad], axis=-1),
             "ctx": jnp.zeros((l_ctx, 128), F32)}
    od_sb = {"lat": jnp.concatenate([-sin_d, jnp.zeros((n, hr), F32), zpad], axis=-1),
             "ctx": jnp.zeros((l_ctx, 128), F32)}

    perm_b = _deinterleave(B_HEAD_DIM)
    perm_d = _deinterleave(MLA_ROPE)

    xs = {"lat": x, "ctx": ctx}
    tms = {"lat": tm_lat, "ctx": tm_ctx}
    for layer in range(depth):
        more = layer < depth - 1
        i = layer // 2
        mod = {"lat": mods[layer, :b], "ctx": mods[layer, b:b + 1]}
        g_pre1, g_post1, g_pre2, g_post2 = (norm_g[layer, j][None, :] for j in range(4))
        streams = ("lat", "ctx")
        if layer % 2 == 0:
            w = ev_w_in[i]
            wq = w[:, EV_A_END:EV_Q_END].reshape(d, B_HEADS, B_HEAD_DIM)[:, :, perm_b].reshape(d, -1)
            wk = w[:, EV_Q_END:EV_K_END].reshape(d, B_KV_HEADS, B_HEAD_DIM)[:, :, perm_b].reshape(d, -1)
            w_in = jnp.concatenate([w[:, :EV_A_END], wq, wk, w[:, EV_K_END:]], axis=1).astype(BF16)
            ln_g, ln_b = ev_a_ln[i, 0][None, :], ev_a_ln[i, 1][None, :]
            w_s = ev_w_s[i].astype(BF16)
            b_s = jnp.broadcast_to(ev_b_s[i][:, :, None], (A_GROUPS, CHUNK, A_GROUP_DIM))
            gq, gk = ev_qk_g[i, 0][perm_b][None, :], ev_qk_g[i, 1][perm_b][None, :]
            w_out = ev_w_out[i].astype(BF16)
            pr = {s: _ev_in(xs[s], mod[s], g_pre1, w_in, ln_g, ln_b, w_s, b_s, gq, gk,
                            ev_cos[s], ev_sin[s], tm_in[s]) for s in streams}
            geom = dict(nh=B_HEADS // B_KV_HEADS, nkv=1, dq=B_HEAD_DIM, dv=B_HEAD_DIM, tq=1024)
            conv_params = None
        else:
            w = od_w_in[i]
            w_in = jnp.concatenate([w[:, :OD_KV_END], w[:, OD_KV_END:][:, perm_d],
                                    jnp.zeros((d, 128 - MLA_ROPE), F32)], axis=1).astype(BF16)
            wuq = od_w_uq[i].reshape(Q_LORA, MLA_HEADS, MLA_NOPE + MLA_ROPE)
            w_uq = jnp.concatenate([wuq[:, :, :MLA_NOPE], wuq[:, :, MLA_NOPE:][:, :, perm_d],
                                    jnp.zeros((Q_LORA, MLA_HEADS, MLA_QK_PAD - MLA_NOPE - MLA_ROPE), F32)],
                                   axis=-1).reshape(Q_LORA, -1).astype(BF16)
            wukv = od_w_ukv[i].reshape(KV_LORA, MLA_HEADS, MLA_NOPE + MLA_V)
            w_ukv = jnp.concatenate([wukv[:, :, :MLA_NOPE].reshape(KV_LORA, -1),
                                     wukv[:, :, MLA_NOPE:].reshape(KV_LORA, -1)], axis=1).astype(BF16)
            w_out = od_w_out[i].astype(BF16)
            pr = {s: _od_in(xs[s], mod[s], g_pre1, w_in, od_g_cq[i][None, :], od_g_ckv[i][None, :],
                            w_uq, w_ukv, od_cos[s], od_sa[s], od_sb[s], tm_in[s]) for s in streams}
            geom = dict(nh=2, nkv=2, dq=MLA_QK_PAD, dv=MLA_V, tq=2048)
            conv_params = (od_conv_w[i], od_conv_b[i][None, :], od_c_ln[i, 0][None, :], od_c_ln[i, 1][None, :])

        fw_in = ffn_w_in[layer].astype(BF16)
        fw_out = ffn_w_out[layer].astype(BF16)
        fcw, fcb = ffn_conv_w[layer], ffn_conv_b[layer][None, :]
        kv_of = {"lat": ("ctx", "lat"), "ctx": ("ctx",)}
        new_xs = {}
        for s in (streams if more else ("lat",)):
            b1, q = pr[s][0], pr[s][1]
            ks = [pr[t][2] for t in kv_of[s]]
            vs = [pr[t][3] for t in kv_of[s]]
            att = _attention(q, ks, vs, **dict(geom, tq=min(geom["tq"], q.shape[1])))
            x1 = _proj(xs[s], mod[s], g_post1, b1, att, w_out, tms[s], conv_params)
            new_xs[s] = _ffn(x1, mod[s], g_pre2, g_post2, fw_in, fcw, fcb, fw_out, tms[s])
        xs = new_xs
    return xs["lat"]
```

```python
import functools

import jax
import jax.numpy as jnp
import numpy as np
from jax import lax
from jax.experimental import pallas as pl
from jax.experimental.pallas import tpu as pltpu

F32 = jnp.float32
BF16 = jnp.bfloat16

GRID_W = 64
ROPE_THETA = 10000.0
EPS = 1e-6

CHUNK = 128
A_GROUPS = 4
A_GROUP_DIM = 128
D_A = A_GROUPS * A_GROUP_DIM
B_HEADS = 8
B_KV_HEADS = 2
B_HEAD_DIM = 128
EV_A_END = 2 * D_A
EV_Q_END = EV_A_END + B_HEADS * B_HEAD_DIM
EV_K_END = EV_Q_END + B_KV_HEADS * B_HEAD_DIM
EV_IN = EV_K_END + B_KV_HEADS * B_HEAD_DIM

D_C = 512
CONV_W = 31
MLA_HEADS = 8
MLA_NOPE = 128
MLA_ROPE = 64
MLA_V = 128
MLA_QK_PAD = 256
Q_LORA = 512
KV_LORA = 256
OD_C_END = 2 * D_C
OD_Q_END = OD_C_END + Q_LORA
OD_KV_END = OD_Q_END + KV_LORA
OD_IN_PAD = OD_KV_END + 128

D_FF = 2816
FFN_CHUNK = 256

ATTN_KEY_CHUNK = 256
ATTN_Q_UNIT = 256
ATTN_SCORE_LEAD = 17
LOG2E = 1.4426950408889634

CONV_HALO = 16
FFN_HALO = 8
IN_SUBTILE = 256

V7X_VMEM_LIMIT = 56 * 1024 * 1024


def _rms(xf, g):
    return xf * lax.rsqrt(jnp.mean(xf * xf, axis=-1, keepdims=True) + EPS) * g


def _layernorm(xf, g, b):
    mu = jnp.mean(xf, axis=-1, keepdims=True)
    xc = xf - mu
    return xc * lax.rsqrt(jnp.mean(xc * xc, axis=-1, keepdims=True) + EPS) * g + b


def _sigmoid(x):
    return 1.0 / (1.0 + jnp.exp(-x))


def _gelu_tanh(x):
    return 0.5 * x * (1.0 + jnp.tanh(np.float32(np.sqrt(2.0 / np.pi)) * (x + 0.044715 * (x * x * x))))


def _dot(a, b):
    return jnp.dot(a, b, preferred_element_type=F32)


def _params(sem):
    return pltpu.CompilerParams(dimension_semantics=sem, vmem_limit_bytes=V7X_VMEM_LIMIT)


def _const_spec(shape):
    nd = len(shape)
    return pl.BlockSpec(shape, lambda *_: (0,) * nd, pipeline_mode=pl.Buffered(1))


def _tok_spec(tm, width):
    return pl.BlockSpec((None, tm, width), lambda b, i: (b, i, 0))


def _mod_spec(mod):
    d = mod.shape[-1]
    if mod.shape[0] == 1:
        return pl.BlockSpec((None, 6, d), lambda b, i: (0, 0, 0))
    return pl.BlockSpec((None, 6, d), lambda b, i: (b, 0, 0))


def _halo_specs(tm, halo, n, width):
    r = tm // halo
    last = n // halo - 1
    prev = pl.BlockSpec((None, halo, width), lambda b, i: (b, jnp.maximum(i * r - 1, 0), 0))
    nxt = pl.BlockSpec((None, halo, width), lambda b, i: (b, jnp.minimum((i + 1) * r, last), 0))
    return prev, nxt


def _mod_kernel(s_ref, w_ref, b_ref, o_ref):
    s = s_ref[...]
    s = s * _sigmoid(s)
    o_ref[...] = jnp.dot(s, w_ref[...], preferred_element_type=F32,
                         precision=lax.Precision.HIGHEST) + b_ref[...]


def _modulation(cc, mod_w, mod_b):
    depth, d, d6 = mod_w.shape
    r = cc.shape[0]
    out = pl.pallas_call(
        _mod_kernel,
        grid=(depth, d6 // d),
        in_specs=[pl.BlockSpec((r, d), lambda l, j: (0, 0)),
                  pl.BlockSpec((None, d, d), lambda l, j: (l, 0, j)),
                  pl.BlockSpec((None, 1, d), lambda l, j: (l, 0, j))],
        out_specs=pl.BlockSpec((None, r, d), lambda l, j: (l, 0, j)),
        out_shape=jax.ShapeDtypeStruct((depth, r, d6), F32),
        compiler_params=_params(("arbitrary", "arbitrary")),
        name="modulation",
    )(cc, mod_w, mod_b.reshape(depth, 1, d6))
    return out.reshape(depth, r, 6, d)


def _ev_in_kernel(x_ref, mod_ref, g_ref, w_ref, lng_ref, lnb_ref, ws_ref, bs_ref, gq_ref, gk_ref,
                  cos_ref, sin_ref, a_ref, q_ref, k_ref, v_ref, *, tm):
    scale = np.float32(B_HEAD_DIM ** -0.5 * LOG2E)
    sub = min(tm, IN_SUBTILE)
    for r0 in range(0, tm, sub):
        rs = slice(r0, r0 + sub)
        h = _rms(x_ref[rs, :], g_ref[...]) * (1.0 + mod_ref[1:2, :]) + mod_ref[0:1, :]
        z = _dot(h.astype(BF16), w_ref[...])

        ga = _gelu_tanh(z[:, :EV_A_END])
        u = ga[:, :D_A]
        vv = _layernorm(ga[:, D_A:], lng_ref[...], lnb_ref[...]).astype(BF16)
        for c in range(sub // CHUNK):
            rows = slice(c * CHUNK, (c + 1) * CHUNK)
            for g in range(A_GROUPS):
                cols = slice(g * A_GROUP_DIM, (g + 1) * A_GROUP_DIM)
                s = _dot(ws_ref[g], vv[rows, cols]) + bs_ref[g]
                a_ref[r0 + c * CHUNK:r0 + (c + 1) * CHUNK, cols] = (u[rows, cols] * s).astype(a_ref.dtype)

        cosf = cos_ref[rs, :]
        sinf = sin_ref[rs, :]

        def head(zh, g, cosf=cosf, sinf=sinf):
            y = _rms(zh, g)
            return y * cosf + pltpu.roll(y, B_HEAD_DIM // 2, 1) * sinf

        for hd in range(B_HEADS):
            cols = slice(hd * B_HEAD_DIM, (hd + 1) * B_HEAD_DIM)
            zh = z[:, EV_A_END + hd * B_HEAD_DIM:EV_A_END + (hd + 1) * B_HEAD_DIM]
            q_ref[rs, cols] = (head(zh, gq_ref[...]) * scale).astype(q_ref.dtype)
        for hd in range(B_KV_HEADS):
            cols = slice(hd * B_HEAD_DIM, (hd + 1) * B_HEAD_DIM)
            zh = z[:, EV_Q_END + hd * B_HEAD_DIM:EV_Q_END + (hd + 1) * B_HEAD_DIM]
            k_ref[rs, cols] = head(zh, gk_ref[...]).astype(k_ref.dtype)
        v_ref[rs, :] = z[:, EV_K_END:].astype(v_ref.dtype)


def _ev_in(x, mod, g_pre, w_in, ln_g, ln_b, w_s, b_s, gq, gk, cosf, sinf, tm):
    b, n, d = x.shape
    kvw = B_KV_HEADS * B_HEAD_DIM
    qw = B_HEADS * B_HEAD_DIM
    return pl.pallas_call(
        functools.partial(_ev_in_kernel, tm=tm),
        grid=(b, n // tm),
        in_specs=[_tok_spec(tm, d), _mod_spec(mod), _const_spec((1, d)), _const_spec(w_in.shape),
                  _const_spec((1, D_A)), _const_spec((1, D_A)), _const_spec(w_s.shape), _const_spec(b_s.shape),
                  _const_spec((1, B_HEAD_DIM)), _const_spec((1, B_HEAD_DIM)),
                  pl.BlockSpec((tm, B_HEAD_DIM), lambda bb, i: (i, 0)),
                  pl.BlockSpec((tm, B_HEAD_DIM), lambda bb, i: (i, 0))],
        out_specs=[_tok_spec(tm, D_A), _tok_spec(tm, qw), _tok_spec(tm, kvw), _tok_spec(tm, kvw)],
        out_shape=[jax.ShapeDtypeStruct((b, n, D_A), BF16), jax.ShapeDtypeStruct((b, n, qw), BF16),
                   jax.ShapeDtypeStruct((b, n, kvw), BF16), jax.ShapeDtypeStruct((b, n, kvw), BF16)],
        compiler_params=_params(("parallel", "parallel")),
        name="ev_in",
    )(x, mod, g_pre, w_in, ln_g, ln_b, w_s, b_s, gq, gk, cosf, sinf)


def _od_in_kernel(x_ref, mod_ref, g_ref, w_ref, gcq_ref, gckv_ref, wuq_ref, wukv_ref,
                  cos_ref, sa_ref, sb_ref, glu_ref, q_ref, k_ref, v_ref, *, tm):
    scale = np.float32((MLA_NOPE + MLA_ROPE) ** -0.5 * LOG2E)
    sub = min(tm, IN_SUBTILE)
    for r0 in range(0, tm, sub):
        rs = slice(r0, r0 + sub)
        h = _rms(x_ref[rs, :], g_ref[...]) * (1.0 + mod_ref[1:2, :]) + mod_ref[0:1, :]
        z = _dot(h.astype(BF16), w_ref[...])

        glu_ref[rs, :] = z[:, :D_C] * _sigmoid(z[:, D_C:OD_C_END])

        cosf = cos_ref[rs, :]
        sa = sa_ref[rs, :]
        sb = sb_ref[rs, :]

        def rope(y, cosf=cosf, sa=sa, sb=sb):
            return y * cosf + pltpu.roll(y, MLA_ROPE // 2, 1) * sa + pltpu.roll(y, 128 - MLA_ROPE // 2, 1) * sb

        cq = _rms(z[:, OD_C_END:OD_Q_END], gcq_ref[...]).astype(BF16)
        qp = _dot(cq, wuq_ref[...])
        ckv = _rms(z[:, OD_Q_END:OD_KV_END], gckv_ref[...]).astype(BF16)
        kv = _dot(ckv, wukv_ref[...])
        kr = rope(z[:, OD_KV_END:OD_IN_PAD]).astype(k_ref.dtype)
        for hd in range(MLA_HEADS):
            lo = hd * MLA_QK_PAD
            mid = lo + MLA_NOPE
            hi = lo + MLA_QK_PAD
            q_ref[rs, lo:mid] = (qp[:, lo:mid] * scale).astype(q_ref.dtype)
            q_ref[rs, mid:hi] = (rope(qp[:, mid:hi]) * scale).astype(q_ref.dtype)
            k_ref[rs, lo:mid] = kv[:, hd * MLA_NOPE:(hd + 1) * MLA_NOPE].astype(k_ref.dtype)
            k_ref[rs, mid:hi] = kr
        v_ref[rs, :] = kv[:, MLA_HEADS * MLA_NOPE:].astype(v_ref.dtype)


def _od_in(x, mod, g_pre, w_in, g_cq, g_ckv, w_uq, w_ukv, cosf, sa, sb, tm):
    b, n, d = x.shape
    qkw = MLA_HEADS * MLA_QK_PAD
    vw = MLA_HEADS * MLA_V
    tab = pl.BlockSpec((tm, 128), lambda bb, i: (i, 0))
    return pl.pallas_call(
        functools.partial(_od_in_kernel, tm=tm),
        grid=(b, n // tm),
        in_specs=[_tok_spec(tm, d), _mod_spec(mod), _const_spec((1, d)), _const_spec(w_in.shape),
                  _const_spec((1, Q_LORA)), _const_spec((1, KV_LORA)), _const_spec(w_uq.shape),
                  _const_spec(w_ukv.shape), tab, tab, tab],
        out_specs=[_tok_spec(tm, D_C), _tok_spec(tm, qkw), _tok_spec(tm, qkw), _tok_spec(tm, vw)],
        out_shape=[jax.ShapeDtypeStruct((b, n, D_C), F32), jax.ShapeDtypeStruct((b, n, qkw), BF16),
                   jax.ShapeDtypeStruct((b, n, qkw), BF16), jax.ShapeDtypeStruct((b, n, vw), BF16)],
        compiler_params=_params(("parallel", "parallel")),
        name="od_in",
    )(x, mod, g_pre, w_in, g_cq, g_ckv, w_uq, w_ukv, cosf, sa, sb)


def _attn_kernel(*refs, nseg, nh, nkv, dq, dv):
    q_ref = refs[0]
    k_refs = refs[1:1 + nseg]
    v_refs = refs[1 + nseg:1 + 2 * nseg]
    o_ref, s_ref, vt_ref = refs[1 + 2 * nseg:]
    tq = q_ref.shape[0]
    kc = ATTN_KEY_CHUNK
    seg_len = [k.shape[0] for k in k_refs]
    seg_off = [sum(seg_len[:i]) for i in range(nseg)]
    n_chunks = sum(seg_len) // kc

    @pl.when(pl.program_id(2) == 0)
    def _():
        for j in range(nkv):
            for v_ref, off, ln in zip(v_refs, seg_off, seg_len):
                for c in range(ln // kc):
                    blk = v_ref[c * kc:(c + 1) * kc, j * dv:(j + 1) * dv]
                    vt_ref[j, off // kc + c] = blk.astype(F32).T.astype(BF16)

    chunk_src = [(k_ref, c * kc) for k_ref, ln in zip(k_refs, seg_len) for c in range(ln // kc)]
    tu = ATTN_Q_UNIT
    units = [(h, r) for h in range(nh) for r in range(tq // tu)]

    assert dv <= kc and dq <= kc
    kpad = kc - dq

    def load_qt(u):
        h, r = units[u]
        qt = q_ref[r * tu:(r + 1) * tu, h * dq:(h + 1) * dq].astype(F32).T.astype(BF16)
        return qt if kpad == 0 else jnp.concatenate([qt, jnp.zeros((kpad, tu), BF16)], axis=0)

    def score_chunk(u, qt, c, m8):
        j = units[u][0] if nkv > 1 else 0
        k_ref, r0 = chunk_src[c]
        for r1 in range(0, kc, dv):
            kk = k_ref[r0 + r1:r0 + r1 + dv, j * dq:(j + 1) * dq]
            if kpad:
                kk = jnp.concatenate([kk, jnp.zeros((dv, kpad), BF16)], axis=1)
            s = _dot(kk, qt)
            s_ref[u % 3, c * kc + r1:c * kc + r1 + dv, :] = s
            s8 = jnp.max(s.reshape(dv // 8, 8, tu), axis=0)
            m8 = s8 if m8 is None else jnp.maximum(m8, s8)
        return m8

    n_steps = len(units) * n_chunks
    state = {"m8": None, "qt": None, "mx": {}}

    def score_step(t):
        u, c = divmod(t, n_chunks)
        if c == 0:
            state["qt"] = load_qt(u)
            state["m8"] = None
        state["m8"] = score_chunk(u, state["qt"], c, state["m8"])
        if c == n_chunks - 1:
            state["mx"][u] = jnp.max(state["m8"], axis=0, keepdims=True)

    lead = n_chunks + min(ATTN_SCORE_LEAD, n_chunks)
    for t in range(min(lead, n_steps)):
        score_step(t)
    l8 = o = None
    for t in range(n_steps):
        u, c = divmod(t, n_chunks)
        h, r = units[u]
        j = h if nkv > 1 else 0
        p = jnp.exp2(s_ref[u % 3, c * kc:(c + 1) * kc, :] - state["mx"][u])
        p8 = jnp.sum(p.reshape(kc // 8, 8, tu), axis=0)
        oc = _dot(vt_ref[j, c], p.astype(BF16))
        l8 = p8 if c == 0 else l8 + p8
        o = oc if c == 0 else o + oc
        if t + lead < n_steps:
            score_step(t + lead)
        if c == n_chunks - 1:
            o = o * (1.0 / jnp.sum(l8, axis=0, keepdims=True))
            o_ref[r * tu:(r + 1) * tu, h * dv:(h + 1) * dv] = o.T.astype(o_ref.dtype)


def _attention(q, ks, vs, nh, nkv, dq, dv, tq):
    b, n, qw = q.shape
    nseg = len(ks)
    groups = qw // (nh * dq)
    m_total = sum(k.shape[1] for k in ks)
    kv_specs = ([pl.BlockSpec((None, k.shape[1], nkv * dq), lambda bb, g, i: (bb, 0, g)) for k in ks]
                + [pl.BlockSpec((None, v.shape[1], nkv * dv), lambda bb, g, i: (bb, 0, g)) for v in vs])
    return pl.pallas_call(
        functools.partial(_attn_kernel, nseg=nseg, nh=nh, nkv=nkv, dq=dq, dv=dv),
        grid=(b, groups, n // tq),
        in_specs=[pl.BlockSpec((None, tq, nh * dq), lambda bb, g, i: (bb, i, g))] + kv_specs,
        out_specs=pl.BlockSpec((None, tq, nh * dv), lambda bb, g, i: (bb, i, g)),
        out_shape=jax.ShapeDtypeStruct((b, n, groups * nh * dv), BF16),
        scratch_shapes=[pltpu.VMEM((3, m_total, ATTN_Q_UNIT), F32),
                        pltpu.VMEM((nkv, m_total // ATTN_KEY_CHUNK, dv, ATTN_KEY_CHUNK), BF16)],
        compiler_params=_params(("parallel", "parallel", "arbitrary")),
        name="attention",
    )(q, *ks, *vs)


def _conformer_tail(main_ref, prev_ref, next_ref, cw_ref, cb_ref, lng_ref, lnb_ref, tm):
    i = pl.program_id(1)
    prev = jnp.where(i > 0, prev_ref[...], 0.0)
    nxt = jnp.where(i < pl.num_programs(1) - 1, next_ref[...], 0.0)
    ext = jnp.concatenate([prev, main_ref[...], nxt], axis=0)
    half = CONV_W // 2
    acc = None
    for r in range(8):
        rolled = ext if r == 0 else pltpu.roll(ext, r, 0)
        for a in range(-2, 2):
            tap = half - (8 * a + r)
            if 0 <= tap < CONV_W:
                lo = CONV_HALO - 8 * a
                term = cw_ref[tap:tap + 1, :] * rolled[lo:lo + tm]
                acc = term if acc is None else acc + term
    hc = _layernorm(acc + cb_ref[...], lng_ref[...], lnb_ref[...])
    return hc * _sigmoid(hc)


def _proj_kernel(*refs, conv, tm, w1):
    if conv:
        (x_ref, mod_ref, g_ref, b1_ref, b1p_ref, b1n_ref, att_ref, w_ref,
         cw_ref, cb_ref, lng_ref, lnb_ref, o_ref) = refs
        b1 = _conformer_tail(b1_ref, b1p_ref, b1n_ref, cw_ref, cb_ref, lng_ref, lnb_ref, tm).astype(BF16)
    else:
        x_ref, mod_ref, g_ref, b1_ref, att_ref, w_ref, o_ref = refs
        b1 = b1_ref[...]
    o = _dot(b1, w_ref[:w1, :]) + _dot(att_ref[...], w_ref[w1:, :])
    o_ref[...] = x_ref[...] + mod_ref[2:3, :] * _rms(o, g_ref[...])


def _proj(x, mod, g_post, b1, att, w_out, tm, conv_params=None):
    b, n, d = x.shape
    w1 = b1.shape[-1]
    conv = conv_params is not None
    in_specs = [_tok_spec(tm, d), _mod_spec(mod), _const_spec((1, d)), _tok_spec(tm, w1)]
    args = [x, mod, g_post, b1]
    if conv:
        in_specs += list(_halo_specs(tm, CONV_HALO, n, w1))
        args += [b1, b1]
    in_specs += [_tok_spec(tm, att.shape[-1]), _const_spec(w_out.shape)]
    args += [att, w_out]
    if conv:
        in_specs += [_const_spec(p.shape) for p in conv_params]
        args += list(conv_params)
    return pl.pallas_call(
        functools.partial(_proj_kernel, conv=conv, tm=tm, w1=w1),
        grid=(b, n // tm),
        in_specs=in_specs,
        out_specs=_tok_spec(tm, d),
        out_shape=jax.ShapeDtypeStruct((b, n, d), F32),
        compiler_params=_params(("parallel", "parallel")),
        name="proj_conv" if conv else "proj",
    )(*args)


def _ffn_kernel(x_ref, xp_ref, xn_ref, mod_ref, g2_ref, gp2_ref, win_ref, cw_ref, cb_ref, wout_ref,
                o_ref, act_ref, *, tm):
    i = pl.program_id(1)
    shift = mod_ref[3:4, :]
    scale1 = 1.0 + mod_ref[4:5, :]

    def pre(xx):
        return _rms(xx, g2_ref[...]) * scale1 + shift

    x = x_ref[...]
    hm = pre(x)
    hp = jnp.where(i > 0, pre(xp_ref[...]), 0.0)
    hn = jnp.where(i < pl.num_programs(1) - 1, pre(xn_ref[...]), 0.0)
    hext = jnp.concatenate([hp, hm, hn], axis=0).astype(BF16)
    hmb = hm.astype(BF16)
    rows = tm + 2 * FFN_HALO
    for c in range(D_FF // FFN_CHUNK):
        cols = slice(c * FFN_CHUNK, (c + 1) * FFN_CHUNK)
        zg = _dot(hext, win_ref[:, cols])
        zu = _dot(hmb, win_ref[:, D_FF + c * FFN_CHUNK:D_FF + (c + 1) * FFN_CHUNK])
        g = (cw_ref[0:1, cols] * pltpu.roll(zg, 1, 0)[FFN_HALO:FFN_HALO + tm]
             + cw_ref[1:2, cols] * zg[FFN_HALO:FFN_HALO + tm]
             + cw_ref[2:3, cols] * pltpu.roll(zg, rows - 1, 0)[FFN_HALO:FFN_HALO + tm]
             + cb_ref[:, cols])
        act_ref[:, cols] = (g * _sigmoid(g) * zu).astype(BF16)
    f = _dot(act_ref[...], wout_ref[...])
    o_ref[...] = x + mod_ref[5:6, :] * _rms(f, gp2_ref[...])


def _ffn(x, mod, g_pre2, g_post2, w_in, conv_w, conv_b, w_out, tm):
    b, n, d = x.shape
    prev, nxt = _halo_specs(tm, FFN_HALO, n, d)
    return pl.pallas_call(
        functools.partial(_ffn_kernel, tm=tm),
        grid=(b, n // tm),
        in_specs=[_tok_spec(tm, d), prev, nxt, _mod_spec(mod), _const_spec((1, d)), _const_spec((1, d)),
                  _const_spec(w_in.shape), _const_spec(conv_w.shape), _const_spec(conv_b.shape),
                  _const_spec(w_out.shape)],
        out_specs=_tok_spec(tm, d),
        out_shape=jax.ShapeDtypeStruct((b, n, d), F32),
        scratch_shapes=[pltpu.VMEM((tm, D_FF), BF16)],
        compiler_params=_params(("parallel", "parallel")),
        name="conv_ffn",
    )(x, x, x, mod, g_pre2, g_post2, w_in, conv_w, conv_b, w_out)


def _deinterleave(d):
    return np.concatenate([np.arange(0, d, 2), np.arange(1, d, 2)])


def _rope_angles(n, d_rot):
    t = jnp.arange(n)
    row = (t // GRID_W).astype(F32)
    col = (t % GRID_W).astype(F32)
    n_freq = d_rot // 4
    freq = ROPE_THETA ** (-jnp.arange(n_freq, dtype=F32) / n_freq)
    ang = jnp.concatenate([row[:, None] * freq, col[:, None] * freq], axis=-1)
    return jnp.cos(ang), jnp.sin(ang)


def kernel(x, c, ctx, c_ctx, mod_w, mod_b, norm_g, ffn_w_in, ffn_conv_w, ffn_conv_b, ffn_w_out,
           ev_w_in, ev_a_ln, ev_w_s, ev_b_s, ev_qk_g, ev_w_out,
           od_w_in, od_conv_w, od_conv_b, od_c_ln, od_g_cq, od_g_ckv, od_w_uq, od_w_ukv, od_w_out):
    b, n, d = x.shape
    l_ctx = ctx.shape[1]
    depth = mod_w.shape[0]
    tm_lat, tm_ctx = 512, l_ctx
    tm_in = {"lat": 1024, "ctx": l_ctx}

    n_rows = -(-(b + 1) // 8) * 8
    cc = jnp.concatenate([c, c_ctx[None, :], jnp.zeros((n_rows - b - 1, d), F32)], axis=0)
    mods = _modulation(cc, mod_w, mod_b)

    cos_b, sin_b = _rope_angles(n, B_HEAD_DIM)
    ev_cos = {"lat": jnp.concatenate([cos_b, cos_b], axis=-1), "ctx": jnp.ones((l_ctx, B_HEAD_DIM), F32)}
    ev_sin = {"lat": jnp.concatenate([-sin_b, sin_b], axis=-1), "ctx": jnp.zeros((l_ctx, B_HEAD_DIM), F32)}
    cos_d, sin_d = _rope_angles(n, MLA_ROPE)
    hr = MLA_ROPE // 2
    zpad = jnp.zeros((n, 128 - MLA_ROPE), F32)
    od_cos = {"lat": jnp.concatenate([cos_d, cos_d, zpad], axis=-1),
              "ctx": jnp.concatenate([jnp.ones((l_ctx, MLA_ROPE), F32), jnp.zeros((l_ctx, 128 - MLA_ROPE), F32)], -1)}
    od_sa = {"lat": jnp.concatenate([jnp.zeros((n, hr), F32), sin_d, zpad], axis=-1),
             "ctx": jnp.zeros((l_ctx, 128), F32)}
    od_sb = {"lat": jnp.concatenate([-sin_d, jnp.zeros((n, hr), F32), zpad], axis=-1),
             "ctx": jnp.zeros((l_ctx, 128), F32)}

    perm_b = _deinterleave(B_HEAD_DIM)
    perm_d = _deinterleave(MLA_ROPE)

    xs = {"lat": x, "ctx": ctx}
    tms = {"lat": tm_lat, "ctx": tm_ctx}
    for layer in range(depth):
        more = layer < depth - 1
        i = layer // 2
        mod = {"lat": mods[layer, :b], "ctx": mods[layer, b:b + 1]}
        g_pre1, g_post1, g_pre2, g_post2 = (norm_g[layer, j][None, :] for j in range(4))
        streams = ("lat", "ctx")
        if layer % 2 == 0:
            w = ev_w_in[i]
            wq = w[:, EV_A_END:EV_Q_END].reshape(d, B_HEADS, B_HEAD_DIM)[:, :, perm_b].reshape(d, -1)
            wk = w[:, EV_Q_END:EV_K_END].reshape(d, B_KV_HEADS, B_HEAD_DIM)[:, :, perm_b].reshape(d, -1)
            w_in = jnp.concatenate([w[:, :EV_A_END], wq, wk, w[:, EV_K_END:]], axis=1).astype(BF16)
            ln_g, ln_b = ev_a_ln[i, 0][None, :], ev_a_ln[i, 1][None, :]
            w_s = ev_w_s[i].astype(BF16)
            b_s = jnp.broadcast_to(ev_b_s[i][:, :, None], (A_GROUPS, CHUNK, A_GROUP_DIM))
            gq, gk = ev_qk_g[i, 0][perm_b][None, :], ev_qk_g[i, 1][perm_b][None, :]
            w_out = ev_w_out[i].astype(BF16)
            pr = {s: _ev_in(xs[s], mod[s], g_pre1, w_in, ln_g, ln_b, w_s, b_s, gq, gk,
                            ev_cos[s], ev_sin[s], tm_in[s]) for s in streams}
            geom = dict(nh=B_HEADS // B_KV_HEADS, nkv=1, dq=B_HEAD_DIM, dv=B_HEAD_DIM, tq=1024)
            conv_params = None
        else:
            w = od_w_in[i]
            w_in = jnp.concatenate([w[:, :OD_KV_END], w[:, OD_KV_END:][:, perm_d],
                                    jnp.zeros((d, 128 - MLA_ROPE), F32)], axis=1).astype(BF16)
            wuq = od_w_uq[i].reshape(Q_LORA, MLA_HEADS, MLA_NOPE + MLA_ROPE)
            w_uq = jnp.concatenate([wuq[:, :, :MLA_NOPE], wuq[:, :, MLA_NOPE:][:, :, perm_d],
                                    jnp.zeros((Q_LORA, MLA_HEADS, MLA_QK_PAD - MLA_NOPE - MLA_ROPE), F32)],
                                   axis=-1).reshape(Q_LORA, -1).astype(BF16)
            wukv = od_w_ukv[i].reshape(KV_LORA, MLA_HEADS, MLA_NOPE + MLA_V)
            w_ukv = jnp.concatenate([wukv[:, :, :MLA_NOPE].reshape(KV_LORA, -1),
                                     wukv[:, :, MLA_NOPE:].reshape(KV_LORA, -1)], axis=1).astype(BF16)
            w_out = od_w_out[i].astype(BF16)
            pr = {s: _od_in(xs[s], mod[s], g_pre1, w_in, od_g_cq[i][None, :], od_g_ckv[i][None, :],
                            w_uq, w_ukv, od_cos[s], od_sa[s], od_sb[s], tm_in[s]) for s in streams}
            geom = dict(nh=2, nkv=2, dq=MLA_QK_PAD, dv=MLA_V, tq=2048)
            conv_params = (od_conv_w[i], od_conv_b[i][None, :], od_c_ln[i, 0][None, :], od_c_ln[i, 1][None, :])

        fw_in = ffn_w_in[layer].astype(BF16)
        fw_out = ffn_w_out[layer].astype(BF16)
        fcw, fcb = ffn_conv_w[layer], ffn_conv_b[layer][None, :]
        kv_of = {"lat": ("ctx", "lat"), "ctx": ("ctx",)}
        new_xs = {}
        for s in (streams if more else ("lat",)):
            b1, q = pr[s][0], pr[s][1]
            ks = [pr[t][2] for t in kv_of[s]]
            vs = [pr[t][3] for t in kv_of[s]]
            att = _attention(q, ks, vs, **dict(geom, tq=min(geom["tq"], q.shape[1])))
            x1 = _proj(xs[s], mod[s], g_post1, b1, att, w_out, tms[s], conv_params)
            new_xs[s] = _ffn(x1, mod[s], g_pre2, g_post2, fw_in, fcw, fcb, fw_out, tms[s])
        xs = new_xs
    return xs["lat"]
```

```python
import functools

import jax
import jax.numpy as jnp
import numpy as np
from jax import lax
from jax.experimental import pallas as pl
from jax.experimental.pallas import tpu as pltpu

F32 = jnp.float32
BF16 = jnp.bfloat16

GRID_W = 64
ROPE_THETA = 10000.0
EPS = 1e-6

CHUNK = 128
A_GROUPS = 4
A_GROUP_DIM = 128
D_A = A_GROUPS * A_GROUP_DIM
B_HEADS = 8
B_KV_HEADS = 2
B_HEAD_DIM = 128
EV_A_END = 2 * D_A
EV_Q_END = EV_A_END + B_HEADS * B_HEAD_DIM
EV_K_END = EV_Q_END + B_KV_HEADS * B_HEAD_DIM
EV_IN = EV_K_END + B_KV_HEADS * B_HEAD_DIM

D_C = 512
CONV_W = 31
MLA_HEADS = 8
MLA_NOPE = 128
MLA_ROPE = 64
MLA_V = 128
MLA_QK_PAD = 256
Q_LORA = 512
KV_LORA = 256
OD_C_END = 2 * D_C
OD_Q_END = OD_C_END + Q_LORA
OD_KV_END = OD_Q_END + KV_LORA
OD_IN_PAD = OD_KV_END + 128

D_FF = 2816
FFN_CHUNK = 256

ATTN_KEY_CHUNK = 256
ATTN_Q_UNIT = 256
ATTN_SCORE_LEAD = 17
LOG2E = 1.4426950408889634

CONV_HALO = 16
FFN_HALO = 8
IN_SUBTILE = 256

V7X_VMEM_LIMIT = 56 * 1024 * 1024


def _rms(xf, g):
    return xf * lax.rsqrt(jnp.mean(xf * xf, axis=-1, keepdims=True) + EPS) * g


def _layernorm(xf, g, b):
    mu = jnp.mean(xf, axis=-1, keepdims=True)
    xc = xf - mu
    return xc * lax.rsqrt(jnp.mean(xc * xc, axis=-1, keepdims=True) + EPS) * g + b


def _sigmoid(x):
    return 1.0 / (1.0 + jnp.exp(-x))


def _gelu_tanh(x):
    return 0.5 * x * (1.0 + jnp.tanh(np.float32(np.sqrt(2.0 / np.pi)) * (x + 0.044715 * (x * x * x))))


def _dot(a, b):
    return jnp.dot(a, b, preferred_element_type=F32)


def _params(sem):
    return pltpu.CompilerParams(dimension_semantics=sem, vmem_limit_bytes=V7X_VMEM_LIMIT)


def _const_spec(shape):
    nd = len(shape)
    return pl.BlockSpec(shape, lambda *_: (0,) * nd, pipeline_mode=pl.Buffered(1))


def _tok_spec(tm, width):
    return pl.BlockSpec((None, tm, width), lambda b, i: (b, i, 0))


def _mod_spec(mod):
    d = mod.shape[-1]
    if mod.shape[0] == 1:
        return pl.BlockSpec((None, 6, d), lambda b, i: (0, 0, 0))
    return pl.BlockSpec((None, 6, d), lambda b, i: (b, 0, 0))


def _halo_specs(tm, halo, n, width):
    r = tm // halo
    last = n // halo - 1
    prev = pl.BlockSpec((None, halo, width), lambda b, i: (b, jnp.maximum(i * r - 1, 0), 0))
    nxt = pl.BlockSpec((None, halo, width), lambda b, i: (b, jnp.minimum((i + 1) * r, last), 0))
    return prev, nxt


def _mod_kernel(s_ref, w_ref, b_ref, o_ref):
    s = s_ref[...]
    s = s * _sigmoid(s)
    o_ref[...] = jnp.dot(s, w_ref[...], preferred_element_type=F32,
                         precision=lax.Precision.HIGHEST) + b_ref[...]


def _modulation(cc, mod_w, mod_b):
    depth, d, d6 = mod_w.shape
    r = cc.shape[0]
    out = pl.pallas_call(
        _mod_kernel,
        grid=(depth, d6 // d),
        in_specs=[pl.BlockSpec((r, d), lambda l, j: (0, 0)),
                  pl.BlockSpec((None, d, d), lambda l, j: (l, 0, j)),
                  pl.BlockSpec((None, 1, d), lambda l, j: (l, 0, j))],
        out_specs=pl.BlockSpec((None, r, d), lambda l, j: (l, 0, j)),
        out_shape=jax.ShapeDtypeStruct((depth, r, d6), F32),
        compiler_params=_params(("arbitrary", "arbitrary")),
        name="modulation",
    )(cc, mod_w, mod_b.reshape(depth, 1, d6))
    return out.reshape(depth, r, 6, d)


def _ev_in_kernel(x_ref, mod_ref, g_ref, w_ref, lng_ref, lnb_ref, ws_ref, bs_ref, gq_ref, gk_ref,
                  cos_ref, sin_ref, a_ref, q_ref, k_ref, v_ref, *, tm):
    scale = np.float32(B_HEAD_DIM ** -0.5 * LOG2E)
    sub = min(tm, IN_SUBTILE)
    for r0 in range(0, tm, sub):
        rs = slice(r0, r0 + sub)
        h = _rms(x_ref[rs, :], g_ref[...]) * (1.0 + mod_ref[1:2, :]) + mod_ref[0:1, :]
        z = _dot(h.astype(BF16), w_ref[...])

        ga = _gelu_tanh(z[:, :EV_A_END])
        u = ga[:, :D_A]
        vv = _layernorm(ga[:, D_A:], lng_ref[...], lnb_ref[...]).astype(BF16)
        for c in range(sub // CHUNK):
            rows = slice(c * CHUNK, (c + 1) * CHUNK)
            for g in range(A_GROUPS):
                cols = slice(g * A_GROUP_DIM, (g + 1) * A_GROUP_DIM)
                s = _dot(ws_ref[g], vv[rows, cols]) + bs_ref[g]
                a_ref[r0 + c * CHUNK:r0 + (c + 1) * CHUNK, cols] = (u[rows, cols] * s).astype(a_ref.dtype)

        cosf = cos_ref[rs, :]
        sinf = sin_ref[rs, :]

        def head(zh, g, cosf=cosf, sinf=sinf):
            y = _rms(zh, g)
            return y * cosf + pltpu.roll(y, B_HEAD_DIM // 2, 1) * sinf

        for hd in range(B_HEADS):
            cols = slice(hd * B_HEAD_DIM, (hd + 1) * B_HEAD_DIM)
            zh = z[:, EV_A_END + hd * B_HEAD_DIM:EV_A_END + (hd + 1) * B_HEAD_DIM]
            q_ref[rs, cols] = (head(zh, gq_ref[...]) * scale).astype(q_ref.dtype)
        for hd in range(B_KV_HEADS):
            cols = slice(hd * B_HEAD_DIM, (hd + 1) * B_HEAD_DIM)
            zh = z[:, EV_Q_END + hd * B_HEAD_DIM:EV_Q_END + (hd + 1) * B_HEAD_DIM]
            k_ref[rs, cols] = head(zh, gk_ref[...]).astype(k_ref.dtype)
        v_ref[rs, :] = z[:, EV_K_END:].astype(v_ref.dtype)


def _ev_in(x, mod, g_pre, w_in, ln_g, ln_b, w_s, b_s, gq, gk, cosf, sinf, tm):
    b, n, d = x.shape
    kvw = B_KV_HEADS * B_HEAD_DIM
    qw = B_HEADS * B_HEAD_DIM
    return pl.pallas_call(
        functools.partial(_ev_in_kernel, tm=tm),
        grid=(b, n // tm),
        in_specs=[_tok_spec(tm, d), _mod_spec(mod), _const_spec((1, d)), _const_spec(w_in.shape),
                  _const_spec((1, D_A)), _const_spec((1, D_A)), _const_spec(w_s.shape), _const_spec(b_s.shape),
                  _const_spec((1, B_HEAD_DIM)), _const_spec((1, B_HEAD_DIM)),
                  pl.BlockSpec((tm, B_HEAD_DIM), lambda bb, i: (i, 0)),
                  pl.BlockSpec((tm, B_HEAD_DIM), lambda bb, i: (i, 0))],
        out_specs=[_tok_spec(tm, D_A), _tok_spec(tm, qw), _tok_spec(tm, kvw), _tok_spec(tm, kvw)],
        out_shape=[jax.ShapeDtypeStruct((b, n, D_A), BF16), jax.ShapeDtypeStruct((b, n, qw), BF16),
                   jax.ShapeDtypeStruct((b, n, kvw), BF16), jax.ShapeDtypeStruct((b, n, kvw), BF16)],
        compiler_params=_params(("parallel", "parallel")),
        name="ev_in",
    )(x, mod, g_pre, w_in, ln_g, ln_b, w_s, b_s, gq, gk, cosf, sinf)


def _od_in_kernel(x_ref, mod_ref, g_ref, w_ref, gcq_ref, gckv_ref, wuq_ref, wukv_ref,
                  cos_ref, sa_ref, sb_ref, glu_ref, q_ref, k_ref, v_ref, *, tm):
    scale = np.float32((MLA_NOPE + MLA_ROPE) ** -0.5 * LOG2E)
    sub = min(tm, IN_SUBTILE)
    for r0 in range(0, tm, sub):
        rs = slice(r0, r0 + sub)
        h = _rms(x_ref[rs, :], g_ref[...]) * (1.0 + mod_ref[1:2, :]) + mod_ref[0:1, :]
        z = _dot(h.astype(BF16), w_ref[...])

        glu_ref[rs, :] = z[:, :D_C] * _sigmoid(z[:, D_C:OD_C_END])

        cosf = cos_ref[rs, :]
        sa = sa_ref[rs, :]
        sb = sb_ref[rs, :]

        def rope(y, cosf=cosf, sa=sa, sb=sb):
            return y * cosf + pltpu.roll(y, MLA_ROPE // 2, 1) * sa + pltpu.roll(y, 128 - MLA_ROPE // 2, 1) * sb

        cq = _rms(z[:, OD_C_END:OD_Q_END], gcq_ref[...]).astype(BF16)
        qp = _dot(cq, wuq_ref[...])
        ckv = _rms(z[:, OD_Q_END:OD_KV_END], gckv_ref[...]).astype(BF16)
        kv = _dot(ckv, wukv_ref[...])
        kr = rope(z[:, OD_KV_END:OD_IN_PAD]).astype(k_ref.dtype)
        for hd in range(MLA_HEADS):
            lo = hd * MLA_QK_PAD
            mid = lo + MLA_NOPE
            hi = lo + MLA_QK_PAD
            q_ref[rs, lo:mid] = (qp[:, lo:mid] * scale).astype(q_ref.dtype)
            q_ref[rs, mid:hi] = (rope(qp[:, mid:hi]) * scale).astype(q_ref.dtype)
            k_ref[rs, lo:mid] = kv[:, hd * MLA_NOPE:(hd + 1) * MLA_NOPE].astype(k_ref.dtype)
            k_ref[rs, mid:hi] = kr
        v_ref[rs, :] = kv[:, MLA_HEADS * MLA_NOPE:].astype(v_ref.dtype)


def _od_in(x, mod, g_pre, w_in, g_cq, g_ckv, w_uq, w_ukv, cosf, sa, sb, tm):
    b, n, d = x.shape
    qkw = MLA_HEADS * MLA_QK_PAD
    vw = MLA_HEADS * MLA_V
    tab = pl.BlockSpec((tm, 128), lambda bb, i: (i, 0))
    return pl.pallas_call(
        functools.partial(_od_in_kernel, tm=tm),
        grid=(b, n // tm),
        in_specs=[_tok_spec(tm, d), _mod_spec(mod), _const_spec((1, d)), _const_spec(w_in.shape),
                  _const_spec((1, Q_LORA)), _const_spec((1, KV_LORA)), _const_spec(w_uq.shape),
                  _const_spec(w_ukv.shape), tab, tab, tab],
        out_specs=[_tok_spec(tm, D_C), _tok_spec(tm, qkw), _tok_spec(tm, qkw), _tok_spec(tm, vw)],
        out_shape=[jax.ShapeDtypeStruct((b, n, D_C), F32), jax.ShapeDtypeStruct((b, n, qkw), BF16),
                   jax.ShapeDtypeStruct((b, n, qkw), BF16), jax.ShapeDtypeStruct((b, n, vw), BF16)],
        compiler_params=_params(("parallel", "parallel")),
        name="od_in",
    )(x, mod, g_pre, w_in, g_cq, g_ckv, w_uq, w_ukv, cosf, sa, sb)


def _attn_kernel(*refs, nseg, nh, nkv, dq, dv):
    q_ref = refs[0]
    k_refs = refs[1:1 + nseg]
    v_refs = refs[1 + nseg:1 + 2 * nseg]
    o_ref, s_ref, vt_ref = refs[1 + 2 * nseg:]
    tq = q_ref.shape[0]
    kc = ATTN_KEY_CHUNK
    seg_len = [k.shape[0] for k in k_refs]
    seg_off = [sum(seg_len[:i]) for i in range(nseg)]
    n_chunks = sum(seg_len) // kc

    @pl.when(pl.program_id(2) == 0)
    def _():
        for j in range(nkv):
            for v_ref, off, ln in zip(v_refs, seg_off, seg_len):
                for c in range(ln // kc):
                    blk = v_ref[c * kc:(c + 1) * kc, j * dv:(j + 1) * dv]
                    vt_ref[j, off // kc + c] = blk.astype(F32).T.astype(BF16)

    chunk_src = [(k_ref, c * kc) for k_ref, ln in zip(k_refs, seg_len) for c in range(ln // kc)]
    tu = ATTN_Q_UNIT
    units = [(h, r) for h in range(nh) for r in range(tq // tu)]

    assert dv <= kc and dq <= kc
    kpad = kc - dq

    def load_qt(u):
        h, r = units[u]
        qt = q_ref[r * tu:(r + 1) * tu, h * dq:(h + 1) * dq].astype(F32).T.astype(BF16)
        return qt if kpad == 0 else jnp.concatenate([qt, jnp.zeros((kpad, tu), BF16)], axis=0)

    def score_chunk(u, qt, c, m8):
        j = units[u][0] if nkv > 1 else 0
        k_ref, r0 = chunk_src[c]
        for r1 in range(0, kc, dv):
            kk = k_ref[r0 + r1:r0 + r1 + dv, j * dq:(j + 1) * dq]
            if kpad:
                kk = jnp.concatenate([kk, jnp.zeros((dv, kpad), BF16)], axis=1)
            s = _dot(kk, qt)
            s_ref[u % 3, c * kc + r1:c * kc + r1 + dv, :] = s
            s8 = jnp.max(s.reshape(dv // 8, 8, tu), axis=0)
            m8 = s8 if m8 is None else jnp.maximum(m8, s8)
        return m8

    n_steps = len(units) * n_chunks
    state = {"m8": None, "qt": None, "mx": {}}

    def score_step(t):
        u, c = divmod(t, n_chunks)
        if c == 0:
            state["qt"] = load_qt(u)
            state["m8"] = None
        state["m8"] = score_chunk(u, state["qt"], c, state["m8"])
        if c == n_chunks - 1:
            state["mx"][u] = jnp.max(state["m8"], axis=0, keepdims=True)

    lead = n_chunks + min(ATTN_SCORE_LEAD, n_chunks)
    for t in range(min(lead, n_steps)):
        score_step(t)
    l8 = o = None
    for t in range(n_steps):
        u, c = divmod(t, n_chunks)
        h, r = units[u]
        j = h if nkv > 1 else 0
        p = jnp.exp2(s_ref[u % 3, c * kc:(c + 1) * kc, :] - state["mx"][u])
        p8 = jnp.sum(p.reshape(kc // 8, 8, tu), axis=0)
        oc = _dot(vt_ref[j, c], p.astype(BF16))
        l8 = p8 if c == 0 else l8 + p8
        o = oc if c == 0 else o + oc
        if t + lead < n_steps:
            score_step(t + lead)
        if c == n_chunks - 1:
            o = o * (1.0 / jnp.sum(l8, axis=0, keepdims=True))
            o_ref[r * tu:(r + 1) * tu, h * dv:(h + 1) * dv] = o.T.astype(o_ref.dtype)


def _attention(q, ks, vs, nh, nkv, dq, dv, tq):
    b, n, qw = q.shape
    nseg = len(ks)
    groups = qw // (nh * dq)
    m_total = sum(k.shape[1] for k in ks)
    kv_specs = ([pl.BlockSpec((None, k.shape[1], nkv * dq), lambda bb, g, i: (bb, 0, g)) for k in ks]
                + [pl.BlockSpec((None, v.shape[1], nkv * dv), lambda bb, g, i: (bb, 0, g)) for v in vs])
    return pl.pallas_call(
        functools.partial(_attn_kernel, nseg=nseg, nh=nh, nkv=nkv, dq=dq, dv=dv),
        grid=(b, groups, n // tq),
        in_specs=[pl.BlockSpec((None, tq, nh * dq), lambda bb, g, i: (bb, i, g))] + kv_specs,
        out_specs=pl.BlockSpec((None, tq, nh * dv), lambda bb, g, i: (bb, i, g)),
        out_shape=jax.ShapeDtypeStruct((b, n, groups * nh * dv), BF16),
        scratch_shapes=[pltpu.VMEM((3, m_total, ATTN_Q_UNIT), F32),
                        pltpu.VMEM((nkv, m_total // ATTN_KEY_CHUNK, dv, ATTN_KEY_CHUNK), BF16)],
        compiler_params=_params(("parallel", "parallel", "arbitrary")),
        name="attention",
    )(q, *ks, *vs)


def _conformer_tail(main_ref, prev_ref, next_ref, cw_ref, cb_ref, lng_ref, lnb_ref, tm):
    i = pl.program_id(1)
    prev = jnp.where(i > 0, prev_ref[...], 0.0)
    nxt = jnp.where(i < pl.num_programs(1) - 1, next_ref[...], 0.0)
    ext = jnp.concatenate([prev, main_ref[...], nxt], axis=0)
    half = CONV_W // 2
    acc = None
    for r in range(8):
        rolled = ext if r == 0 else pltpu.roll(ext, r, 0)
        for a in range(-2, 2):
            tap = half - (8 * a + r)
            if 0 <= tap < CONV_W:
                lo = CONV_HALO - 8 * a
                term = cw_ref[tap:tap + 1, :] * rolled[lo:lo + tm]
                acc = term if acc is None else acc + term
    hc = _layernorm(acc + cb_ref[...], lng_ref[...], lnb_ref[...])
    return hc * _sigmoid(hc)


def _proj_kernel(*refs, conv, tm, w1):
    if conv:
        (x_ref, mod_ref, g_ref, b1_ref, b1p_ref, b1n_ref, att_ref, w_ref,
         cw_ref, cb_ref, lng_ref, lnb_ref, o_ref) = refs
        b1 = _conformer_tail(b1_ref, b1p_ref, b1n_ref, cw_ref, cb_ref, lng_ref, lnb_ref, tm).astype(BF16)
    else:
        x_ref, mod_ref, g_ref, b1_ref, att_ref, w_ref, o_ref = refs
        b1 = b1_ref[...]
    o = _dot(b1, w_ref[:w1, :]) + _dot(att_ref[...], w_ref[w1:, :])
    o_ref[...] = x_ref[...] + mod_ref[2:3, :] * _rms(o, g_ref[...])


def _proj(x, mod, g_post, b1, att, w_out, tm, conv_params=None):
    b, n, d = x.shape
    w1 = b1.shape[-1]
    conv = conv_params is not None
    in_specs = [_tok_spec(tm, d), _mod_spec(mod), _const_spec((1, d)), _tok_spec(tm, w1)]
    args = [x, mod, g_post, b1]
    if conv:
        in_specs += list(_halo_specs(tm, CONV_HALO, n, w1))
        args += [b1, b1]
    in_specs += [_tok_spec(tm, att.shape[-1]), _const_spec(w_out.shape)]
    args += [att, w_out]
    if conv:
        in_specs += [_const_spec(p.shape) for p in conv_params]
        args += list(conv_params)
    return pl.pallas_call(
        functools.partial(_proj_kernel, conv=conv, tm=tm, w1=w1),
        grid=(b, n // tm),
        in_specs=in_specs,
        out_specs=_tok_spec(tm, d),
        out_shape=jax.ShapeDtypeStruct((b, n, d), F32),
        compiler_params=_params(("parallel", "parallel")),
        name="proj_conv" if conv else "proj",
    )(*args)


def _ffn_kernel(x_ref, xp_ref, xn_ref, mod_ref, g2_ref, gp2_ref, win_ref, cw_ref, cb_ref, wout_ref,
                o_ref, act_ref, *, tm):
    i = pl.program_id(1)
    shift = mod_ref[3:4, :]
    scale1 = 1.0 + mod_ref[4:5, :]

    def pre(xx):
        return _rms(xx, g2_ref[...]) * scale1 + shift

    x = x_ref[...]
    hm = pre(x)
    hp = jnp.where(i > 0, pre(xp_ref[...]), 0.0)
    hn = jnp.where(i < pl.num_programs(1) - 1, pre(xn_ref[...]), 0.0)
    hext = jnp.concatenate([hp, hm, hn], axis=0).astype(BF16)
    hmb = hm.astype(BF16)
    rows = tm + 2 * FFN_HALO
    for c in range(D_FF // FFN_CHUNK):
        cols = slice(c * FFN_CHUNK, (c + 1) * FFN_CHUNK)
        zg = _dot(hext, win_ref[:, cols])
        zu = _dot(hmb, win_ref[:, D_FF + c * FFN_CHUNK:D_FF + (c + 1) * FFN_CHUNK])
        g = (cw_ref[0:1, cols] * pltpu.roll(zg, 1, 0)[FFN_HALO:FFN_HALO + tm]
             + cw_ref[1:2, cols] * zg[FFN_HALO:FFN_HALO + tm]
             + cw_ref[2:3, cols] * pltpu.roll(zg, rows - 1, 0)[FFN_HALO:FFN_HALO + tm]
             + cb_ref[:, cols])
        act_ref[:, cols] = (g * _sigmoid(g) * zu).astype(BF16)
    f = _dot(act_ref[...], wout_ref[...])
    o_ref[...] = x + mod_ref[5:6, :] * _rms(f, gp2_ref[...])


def _ffn(x, mod, g_pre2, g_post2, w_in, conv_w, conv_b, w_out, tm):
    b, n, d = x.shape
    prev, nxt = _halo_specs(tm, FFN_HALO, n, d)
    return pl.pallas_call(
        functools.partial(_ffn_kernel, tm=tm),
        grid=(b, n // tm),
        in_specs=[_tok_spec(tm, d), prev, nxt, _mod_spec(mod), _const_spec((1, d)), _const_spec((1, d)),
                  _const_spec(w_in.shape), _const_spec(conv_w.shape), _const_spec(conv_b.shape),
                  _const_spec(w_out.shape)],
        out_specs=_tok_spec(tm, d),
        out_shape=jax.ShapeDtypeStruct((b, n, d), F32),
        scratch_shapes=[pltpu.VMEM((tm, D_FF), BF16)],
        compiler_params=_params(("parallel", "parallel")),
        name="conv_ffn",
    )(x, x, x, mod, g_pre2, g_post2, w_in, conv_w, conv_b, w_out)


def _deinterleave(d):
    return np.concatenate([np.arange(0, d, 2), np.arange(1, d, 2)])


def _rope_angles(n, d_rot):
    t = jnp.arange(n)
    row = (t // GRID_W).astype(F32)
    col = (t % GRID_W).astype(F32)
    n_freq = d_rot // 4
    freq = ROPE_THETA ** (-jnp.arange(n_freq, dtype=F32) / n_freq)
    ang = jnp.concatenate([row[:, None] * freq, col[:, None] * freq], axis=-1)
    return jnp.cos(ang), jnp.sin(ang)


def kernel(x, c, ctx, c_ctx, mod_w, mod_b, norm_g, ffn_w_in, ffn_conv_w, ffn_conv_b, ffn_w_out,
           ev_w_in, ev_a_ln, ev_w_s, ev_b_s, ev_qk_g, ev_w_out,
           od_w_in, od_conv_w, od_conv_b, od_c_ln, od_g_cq, od_g_ckv, od_w_uq, od_w_ukv, od_w_out):
    b, n, d = x.shape
    l_ctx = ctx.shape[1]
    depth = mod_w.shape[0]
    tm_lat, tm_ctx = 512, l_ctx
    tm_in = {"lat": 1024, "ctx": l_ctx}

    n_rows = -(-(b + 1) // 8) * 8
    cc = jnp.concatenate([c, c_ctx[None, :], jnp.zeros((n_rows - b - 1, d), F32)], axis=0)
    mods = _modulation(cc, mod_w, mod_b)

    cos_b, sin_b = _rope_angles(n, B_HEAD_DIM)
    ev_cos = {"lat": jnp.concatenate([cos_b, cos_b], axis=-1), "ctx": jnp.ones((l_ctx, B_HEAD_DIM), F32)}
    ev_sin = {"lat": jnp.concatenate([-sin_b, sin_b], axis=-1), "ctx": jnp.zeros((l_ctx, B_HEAD_DIM), F32)}
    cos_d, sin_d = _rope_angles(n, MLA_ROPE)
    hr = MLA_ROPE // 2
    zpad = jnp.zeros((n, 128 - MLA_ROPE), F32)
    od_cos = {"lat": jnp.concatenate([cos_d, cos_d, zpad], axis=-1),
              "ctx": jnp.concatenate([jnp.ones((l_ctx, MLA_ROPE), F32), jnp.zeros((l_ctx, 128 - MLA_ROPE), F32)], -1)}
    od_sa = {"lat": jnp.concatenate([jnp.zeros((n, hr), F32), sin_d, zpad], axis=-1),
             "ctx": jnp.zeros((l_ctx, 128), F32)}
    od_sb = {"lat": jnp.concatenate([-sin_d, jnp.zeros((n, hr), F32), zpad], axis=-1),
             "ctx": jnp.zeros((l_ctx, 128), F32)}

    perm_b = _deinterleave(B_HEAD_DIM)
    perm_d = _deinterleave(MLA_ROPE)

    xs = {"lat": x, "ctx": ctx}
    tms = {"lat": tm_lat, "ctx": tm_ctx}
    for layer in range(depth):
        more = layer < depth - 1
        i = layer // 2
        mod = {"lat": mods[layer, :b], "ctx": mods[layer, b:b + 1]}
        g_pre1, g_post1, g_pre2, g_post2 = (norm_g[layer, j][None, :] for j in range(4))
        streams = ("lat", "ctx")
        if layer % 2 == 0:
            w = ev_w_in[i]
            wq = w[:, EV_A_END:EV_Q_END].reshape(d, B_HEADS, B_HEAD_DIM)[:, :, perm_b].reshape(d, -1)
            wk = w[:, EV_Q_END:EV_K_END].reshape(d, B_KV_HEADS, B_HEAD_DIM)[:, :, perm_b].reshape(d, -1)
            w_in = jnp.concatenate([w[:, :EV_A_END], wq, wk, w[:, EV_K_END:]], axis=1).astype(BF16)
            ln_g, ln_b = ev_a_ln[i, 0][None, :], ev_a_ln[i, 1][None, :]
            w_s = ev_w_s[i].astype(BF16)
            b_s = jnp.broadcast_to(ev_b_s[i][:, :, None], (A_GROUPS, CHUNK, A_GROUP_DIM))
            gq, gk = ev_qk_g[i, 0][perm_b][None, :], ev_qk_g[i, 1][perm_b][None, :]
            w_out = ev_w_out[i].astype(BF16)
            pr = {s: _ev_in(xs[s], mod[s], g_pre1, w_in, ln_g, ln_b, w_s, b_s, gq, gk,
                            ev_cos[s], ev_sin[s], tm_in[s]) for s in streams}
            geom = dict(nh=B_HEADS // B_KV_HEADS, nkv=1, dq=B_HEAD_DIM, dv=B_HEAD_DIM, tq=2048)
            conv_params = None
        else:
            w = od_w_in[i]
            w_in = jnp.concatenate([w[:, :OD_KV_END], w[:, OD_KV_END:][:, perm_d],
                                    jnp.zeros((d, 128 - MLA_ROPE), F32)], axis=1).astype(BF16)
            wuq = od_w_uq[i].reshape(Q_LORA, MLA_HEADS, MLA_NOPE + MLA_ROPE)
            w_uq = jnp.concatenate([wuq[:, :, :MLA_NOPE], wuq[:, :, MLA_NOPE:][:, :, perm_d],
                                    jnp.zeros((Q_LORA, MLA_HEADS, MLA_QK_PAD - MLA_NOPE - MLA_ROPE), F32)],
                                   axis=-1).reshape(Q_LORA, -1).astype(BF16)
            wukv = od_w_ukv[i].reshape(KV_LORA, MLA_HEADS, MLA_NOPE + MLA_V)
            w_ukv = jnp.concatenate([wukv[:, :, :MLA_NOPE].reshape(KV_LORA, -1),
                                     wukv[:, :, MLA_NOPE:].reshape(KV_LORA, -1)], axis=1).astype(BF16)
            w_out = od_w_out[i].astype(BF16)
            pr = {s: _od_in(xs[s], mod[s], g_pre1, w_in, od_g_cq[i][None, :], od_g_ckv[i][None, :],
                            w_uq, w_ukv, od_cos[s], od_sa[s], od_sb[s], tm_in[s]) for s in streams}
            geom = dict(nh=2, nkv=2, dq=MLA_QK_PAD, dv=MLA_V, tq=2048)
            conv_params = (od_conv_w[i], od_conv_b[i][None, :], od_c_ln[i, 0][None, :], od_c_ln[i, 1][None, :])

        fw_in = ffn_w_in[layer].astype(BF16)
        fw_out = ffn_w_out[layer].astype(BF16)
        fcw, fcb = ffn_conv_w[layer], ffn_conv_b[layer][None, :]
        kv_of = {"lat": ("ctx", "lat"), "ctx": ("ctx",)}
        new_xs = {}
        for s in (streams if more else ("lat",)):
            b1, q = pr[s][0], pr[s][1]
            ks = [pr[t][2] for t in kv_of[s]]
            vs = [pr[t][3] for t in kv_of[s]]
            att = _attention(q, ks, vs, **dict(geom, tq=min(geom["tq"], q.shape[1])))
            x1 = _proj(xs[s], mod[s], g_post1, b1, att, w_out, tms[s], conv_params)
            new_xs[s] = _ffn(x1, mod[s], g_pre2, g_post2, fw_in, fcw, fcb, fw_out, tms[s])
        xs = new_xs
    return xs["lat"]
```

```python
import functools

import jax
import jax.numpy as jnp
import numpy as np
from jax import lax
from jax.experimental import pallas as pl
from jax.experimental.pallas import tpu as pltpu

F32 = jnp.float32
BF16 = jnp.bfloat16

GRID_W = 64
ROPE_THETA = 10000.0
EPS = 1e-6

CHUNK = 128
A_GROUPS = 4
A_GROUP_DIM = 128
D_A = A_GROUPS * A_GROUP_DIM
B_HEADS = 8
B_KV_HEADS = 2
B_HEAD_DIM = 128
EV_A_END = 2 * D_A
EV_Q_END = EV_A_END + B_HEADS * B_HEAD_DIM
EV_K_END = EV_Q_END + B_KV_HEADS * B_HEAD_DIM
EV_IN = EV_K_END + B_KV_HEADS * B_HEAD_DIM

D_C = 512
CONV_W = 31
MLA_HEADS = 8
MLA_NOPE = 128
MLA_ROPE = 64
MLA_V = 128
MLA_QK_PAD = 256
Q_LORA = 512
KV_LORA = 256
OD_C_END = 2 * D_C
OD_Q_END = OD_C_END + Q_LORA
OD_KV_END = OD_Q_END + KV_LORA
OD_IN_PAD = OD_KV_END + 128

D_FF = 2816
FFN_CHUNK = 256

ATTN_KEY_CHUNK = 256
ATTN_Q_UNIT = 256
ATTN_SCORE_BUFFERS = 4
LOG2E = 1.4426950408889634

CONV_HALO = 16
FFN_HALO = 8
IN_SUBTILE = 256

V7X_VMEM_LIMIT = 56 * 1024 * 1024


def _rms(xf, g):
    return xf * lax.rsqrt(jnp.mean(xf * xf, axis=-1, keepdims=True) + EPS) * g


def _layernorm(xf, g, b):
    mu = jnp.mean(xf, axis=-1, keepdims=True)
    xc = xf - mu
    return xc * lax.rsqrt(jnp.mean(xc * xc, axis=-1, keepdims=True) + EPS) * g + b


def _sigmoid(x):
    return 1.0 / (1.0 + jnp.exp(-x))


def _gelu_tanh(x):
    return 0.5 * x * (1.0 + jnp.tanh(np.float32(np.sqrt(2.0 / np.pi)) * (x + 0.044715 * (x * x * x))))


def _dot(a, b):
    return jnp.dot(a, b, preferred_element_type=F32)


def _params(sem):
    return pltpu.CompilerParams(dimension_semantics=sem, vmem_limit_bytes=V7X_VMEM_LIMIT)


def _const_spec(shape):
    nd = len(shape)
    return pl.BlockSpec(shape, lambda *_: (0,) * nd, pipeline_mode=pl.Buffered(1))


def _tok_spec(tm, width):
    return pl.BlockSpec((None, tm, width), lambda b, i: (b, i, 0))


def _mod_spec(mod):
    d = mod.shape[-1]
    if mod.shape[0] == 1:
        return pl.BlockSpec((None, 6, d), lambda b, i: (0, 0, 0))
    return pl.BlockSpec((None, 6, d), lambda b, i: (b, 0, 0))


def _halo_specs(tm, halo, n, width):
    r = tm // halo
    last = n // halo - 1
    prev = pl.BlockSpec((None, halo, width), lambda b, i: (b, jnp.maximum(i * r - 1, 0), 0))
    nxt = pl.BlockSpec((None, halo, width), lambda b, i: (b, jnp.minimum((i + 1) * r, last), 0))
    return prev, nxt


def _mod_kernel(s_ref, w_ref, b_ref, o_ref):
    s = s_ref[...]
    s = s * _sigmoid(s)
    o_ref[...] = jnp.dot(s, w_ref[...], preferred_element_type=F32,
                         precision=lax.Precision.HIGHEST) + b_ref[...]


def _modulation(cc, mod_w, mod_b):
    depth, d, d6 = mod_w.shape
    r = cc.shape[0]
    out = pl.pallas_call(
        _mod_kernel,
        grid=(depth, d6 // d),
        in_specs=[pl.BlockSpec((r, d), lambda l, j: (0, 0)),
                  pl.BlockSpec((None, d, d), lambda l, j: (l, 0, j)),
                  pl.BlockSpec((None, 1, d), lambda l, j: (l, 0, j))],
        out_specs=pl.BlockSpec((None, r, d), lambda l, j: (l, 0, j)),
        out_shape=jax.ShapeDtypeStruct((depth, r, d6), F32),
        compiler_params=_params(("arbitrary", "arbitrary")),
        name="modulation",
    )(cc, mod_w, mod_b.reshape(depth, 1, d6))
    return out.reshape(depth, r, 6, d)


def _ev_in_kernel(x_ref, mod_ref, g_ref, w_ref, lng_ref, lnb_ref, ws_ref, bs_ref, gq_ref, gk_ref,
                  cos_ref, sin_ref, a_ref, q_ref, k_ref, v_ref, *, tm):
    scale = np.float32(B_HEAD_DIM ** -0.5 * LOG2E)
    sub = min(tm, IN_SUBTILE)
    for r0 in range(0, tm, sub):
        rs = slice(r0, r0 + sub)
        h = _rms(x_ref[rs, :], g_ref[...]) * (1.0 + mod_ref[1:2, :]) + mod_ref[0:1, :]
        z = _dot(h.astype(BF16), w_ref[...])

        ga = _gelu_tanh(z[:, :EV_A_END])
        u = ga[:, :D_A]
        vv = _layernorm(ga[:, D_A:], lng_ref[...], lnb_ref[...]).astype(BF16)
        for c in range(sub // CHUNK):
            rows = slice(c * CHUNK, (c + 1) * CHUNK)
            for g in range(A_GROUPS):
                cols = slice(g * A_GROUP_DIM, (g + 1) * A_GROUP_DIM)
                s = _dot(ws_ref[g], vv[rows, cols]) + bs_ref[g]
                a_ref[r0 + c * CHUNK:r0 + (c + 1) * CHUNK, cols] = (u[rows, cols] * s).astype(a_ref.dtype)

        cosf = cos_ref[rs, :]
        sinf = sin_ref[rs, :]

        def head(zh, g, cosf=cosf, sinf=sinf):
            y = _rms(zh, g)
            return y * cosf + pltpu.roll(y, B_HEAD_DIM // 2, 1) * sinf

        for hd in range(B_HEADS):
            cols = slice(hd * B_HEAD_DIM, (hd + 1) * B_HEAD_DIM)
            zh = z[:, EV_A_END + hd * B_HEAD_DIM:EV_A_END + (hd + 1) * B_HEAD_DIM]
            q_ref[rs, cols] = (head(zh, gq_ref[...]) * scale).astype(q_ref.dtype)
        for hd in range(B_KV_HEADS):
            cols = slice(hd * B_HEAD_DIM, (hd + 1) * B_HEAD_DIM)
            zh = z[:, EV_Q_END + hd * B_HEAD_DIM:EV_Q_END + (hd + 1) * B_HEAD_DIM]
            k_ref[rs, cols] = head(zh, gk_ref[...]).astype(k_ref.dtype)
        v_ref[rs, :] = z[:, EV_K_END:].astype(v_ref.dtype)


def _ev_in(x, mod, g_pre, w_in, ln_g, ln_b, w_s, b_s, gq, gk, cosf, sinf, tm):
    b, n, d = x.shape
    kvw = B_KV_HEADS * B_HEAD_DIM
    qw = B_HEADS * B_HEAD_DIM
    return pl.pallas_call(
        functools.partial(_ev_in_kernel, tm=tm),
        grid=(b, n // tm),
        in_specs=[_tok_spec(tm, d), _mod_spec(mod), _const_spec((1, d)), _const_spec(w_in.shape),
                  _const_spec((1, D_A)), _const_spec((1, D_A)), _const_spec(w_s.shape), _const_spec(b_s.shape),
                  _const_spec((1, B_HEAD_DIM)), _const_spec((1, B_HEAD_DIM)),
                  pl.BlockSpec((tm, B_HEAD_DIM), lambda bb, i: (i, 0)),
                  pl.BlockSpec((tm, B_HEAD_DIM), lambda bb, i: (i, 0))],
        out_specs=[_tok_spec(tm, D_A), _tok_spec(tm, qw), _tok_spec(tm, kvw), _tok_spec(tm, kvw)],
        out_shape=[jax.ShapeDtypeStruct((b, n, D_A), BF16), jax.ShapeDtypeStruct((b, n, qw), BF16),
                   jax.ShapeDtypeStruct((b, n, kvw), BF16), jax.ShapeDtypeStruct((b, n, kvw), BF16)],
        compiler_params=_params(("parallel", "parallel")),
        name="ev_in",
    )(x, mod, g_pre, w_in, ln_g, ln_b, w_s, b_s, gq, gk, cosf, sinf)


def _od_in_kernel(x_ref, mod_ref, g_ref, w_ref, gcq_ref, gckv_ref, wuq_ref, wukv_ref,
                  cos_ref, sa_ref, sb_ref, glu_ref, q_ref, k_ref, v_ref, *, tm):
    scale = np.float32((MLA_NOPE + MLA_ROPE) ** -0.5 * LOG2E)
    sub = min(tm, IN_SUBTILE)
    for r0 in range(0, tm, sub):
        rs = slice(r0, r0 + sub)
        h = _rms(x_ref[rs, :], g_ref[...]) * (1.0 + mod_ref[1:2, :]) + mod_ref[0:1, :]
        z = _dot(h.astype(BF16), w_ref[...])

        glu_ref[rs, :] = z[:, :D_C] * _sigmoid(z[:, D_C:OD_C_END])

        cosf = cos_ref[rs, :]
        sa = sa_ref[rs, :]
        sb = sb_ref[rs, :]

        def rope(y, cosf=cosf, sa=sa, sb=sb):
            return y * cosf + pltpu.roll(y, MLA_ROPE // 2, 1) * sa + pltpu.roll(y, 128 - MLA_ROPE // 2, 1) * sb

        cq = _rms(z[:, OD_C_END:OD_Q_END], gcq_ref[...]).astype(BF16)
        qp = _dot(cq, wuq_ref[...])
        ckv = _rms(z[:, OD_Q_END:OD_KV_END], gckv_ref[...]).astype(BF16)
        kv = _dot(ckv, wukv_ref[...])
        kr = rope(z[:, OD_KV_END:OD_IN_PAD]).astype(k_ref.dtype)
        for hd in range(MLA_HEADS):
            lo = hd * MLA_QK_PAD
            mid = lo + MLA_NOPE
            hi = lo + MLA_QK_PAD
            q_ref[rs, lo:mid] = (qp[:, lo:mid] * scale).astype(q_ref.dtype)
            q_ref[rs, mid:hi] = (rope(qp[:, mid:hi]) * scale).astype(q_ref.dtype)
            k_ref[rs, lo:mid] = kv[:, hd * MLA_NOPE:(hd + 1) * MLA_NOPE].astype(k_ref.dtype)
            k_ref[rs, mid:hi] = kr
        v_ref[rs, :] = kv[:, MLA_HEADS * MLA_NOPE:].astype(v_ref.dtype)


def _od_in(x, mod, g_pre, w_in, g_cq, g_ckv, w_uq, w_ukv, cosf, sa, sb, tm):
    b, n, d = x.shape
    qkw = MLA_HEADS * MLA_QK_PAD
    vw = MLA_HEADS * MLA_V
    tab = pl.BlockSpec((tm, 128), lambda bb, i: (i, 0))
    return pl.pallas_call(
        functools.partial(_od_in_kernel, tm=tm),
        grid=(b, n // tm),
        in_specs=[_tok_spec(tm, d), _mod_spec(mod), _const_spec((1, d)), _const_spec(w_in.shape),
                  _const_spec((1, Q_LORA)), _const_spec((1, KV_LORA)), _const_spec(w_uq.shape),
                  _const_spec(w_ukv.shape), tab, tab, tab],
        out_specs=[_tok_spec(tm, D_C), _tok_spec(tm, qkw), _tok_spec(tm, qkw), _tok_spec(tm, vw)],
        out_shape=[jax.ShapeDtypeStruct((b, n, D_C), F32), jax.ShapeDtypeStruct((b, n, qkw), BF16),
                   jax.ShapeDtypeStruct((b, n, qkw), BF16), jax.ShapeDtypeStruct((b, n, vw), BF16)],
        compiler_params=_params(("parallel", "parallel")),
        name="od_in",
    )(x, mod, g_pre, w_in, g_cq, g_ckv, w_uq, w_ukv, cosf, sa, sb)


def _attn_kernel(*refs, nseg, nh, nkv, dq, dv):
    q_ref, qn_ref = refs[0], refs[1]
    refs = refs[1:]
    k_refs = refs[1:1 + nseg]
    v_refs = refs[1 + nseg:1 + 2 * nseg]
    o_ref, s_ref, vt_ref, carry_ref = refs[1 + 2 * nseg:]
    nb = s_ref.shape[0]
    tq = q_ref.shape[0]
    kc = ATTN_KEY_CHUNK
    seg_len = [k.shape[0] for k in k_refs]
    seg_off = [sum(seg_len[:i]) for i in range(nseg)]
    n_chunks = sum(seg_len) // kc

    @pl.when(pl.program_id(2) == 0)
    def _():
        for j in range(nkv):
            for v_ref, off, ln in zip(v_refs, seg_off, seg_len):
                for c in range(ln // kc):
                    blk = v_ref[c * kc:(c + 1) * kc, j * dv:(j + 1) * dv]
                    vt_ref[j, off // kc + c] = blk.astype(F32).T.astype(BF16)

    chunk_src = [(k_ref, c * kc) for k_ref, ln in zip(k_refs, seg_len) for c in range(ln // kc)]
    tu = ATTN_Q_UNIT
    units = [(h, r) for h in range(nh) for r in range(tq // tu)]

    assert dv <= kc and dq <= kc
    kpad = kc - dq

    def load_qt(u, src):
        h, r = units[u]
        qt = src[r * tu:(r + 1) * tu, h * dq:(h + 1) * dq].astype(F32).T.astype(BF16)
        return qt if kpad == 0 else jnp.concatenate([qt, jnp.zeros((kpad, tu), BF16)], axis=0)

    def score_chunk(ug, qt, c, m8):
        j = units[ug % len(units)][0] if nkv > 1 else 0
        k_ref, r0 = chunk_src[c]
        for r1 in range(0, kc, dv):
            kk = k_ref[r0 + r1:r0 + r1 + dv, j * dq:(j + 1) * dq]
            if kpad:
                kk = jnp.concatenate([kk, jnp.zeros((dv, kpad), BF16)], axis=1)
            s = _dot(kk, qt)
            s_ref[ug % nb, c * kc + r1:c * kc + r1 + dv, :] = s
            s8 = jnp.max(s.reshape(dv // 8, 8, tu), axis=0)
            m8 = s8 if m8 is None else jnp.maximum(m8, s8)
        return m8

    n_units = len(units)
    n_steps = n_units * n_chunks
    lead = 2 * n_chunks
    assert n_units % nb == 0 and n_units >= 2
    state = {"m8": None, "qt": None, "mx": {}}

    def score_step(t, first_tile=False):
        ug, c = divmod(t, n_chunks)
        if c == 0:
            state["qt"] = load_qt(ug % n_units, qn_ref if ug >= n_units else q_ref)
            state["m8"] = None
        state["m8"] = score_chunk(ug, state["qt"], c, state["m8"])
        if c == n_chunks - 1:
            if ug >= n_units or first_tile:
                carry_ref[ug % n_units] = state["m8"]
            else:
                state["mx"][ug] = jnp.max(state["m8"], axis=0, keepdims=True)

    @pl.when(pl.program_id(2) == 0)
    def _():
        for t in range(lead):
            score_step(t, first_tile=True)

    for u0 in range(2):
        state["mx"][u0] = jnp.max(carry_ref[u0], axis=0, keepdims=True)
    l8 = o = None
    for t in range(n_steps):
        u, c = divmod(t, n_chunks)
        h, r = units[u]
        j = h if nkv > 1 else 0
        p = jnp.exp2(s_ref[u % nb, c * kc:(c + 1) * kc, :] - state["mx"][u])
        p8 = jnp.sum(p.reshape(kc // 8, 8, tu), axis=0)
        oc = _dot(vt_ref[j, c], p.astype(BF16))
        l8 = p8 if c == 0 else l8 + p8
        o = oc if c == 0 else o + oc
        score_step(t + lead)
        if c == n_chunks - 1:
            o = o * (1.0 / jnp.sum(l8, axis=0, keepdims=True))
            o_ref[r * tu:(r + 1) * tu, h * dv:(h + 1) * dv] = o.T.astype(o_ref.dtype)


def _attention(q, ks, vs, nh, nkv, dq, dv, tq):
    b, n, qw = q.shape
    nseg = len(ks)
    while (nh * (tq // ATTN_Q_UNIT)) % ATTN_SCORE_BUFFERS:
        nh, nkv = nh * 2, (nkv * 2 if nkv > 1 else 1)
    groups = qw // (nh * dq)
    last = n // tq - 1
    m_total = sum(k.shape[1] for k in ks)
    kv_specs = ([pl.BlockSpec((None, k.shape[1], nkv * dq), lambda bb, g, i: (bb, 0, g)) for k in ks]
                + [pl.BlockSpec((None, v.shape[1], nkv * dv), lambda bb, g, i: (bb, 0, g)) for v in vs])
    return pl.pallas_call(
        functools.partial(_attn_kernel, nseg=nseg, nh=nh, nkv=nkv, dq=dq, dv=dv),
        grid=(b, groups, n // tq),
        in_specs=[pl.BlockSpec((None, tq, nh * dq), lambda bb, g, i: (bb, i, g)),
                  pl.BlockSpec((None, tq, nh * dq), lambda bb, g, i: (bb, jnp.minimum(i + 1, last), g))] + kv_specs,
        out_specs=pl.BlockSpec((None, tq, nh * dv), lambda bb, g, i: (bb, i, g)),
        out_shape=jax.ShapeDtypeStruct((b, n, groups * nh * dv), BF16),
        scratch_shapes=[pltpu.VMEM((ATTN_SCORE_BUFFERS, m_total, ATTN_Q_UNIT), F32),
                        pltpu.VMEM((nkv, m_total // ATTN_KEY_CHUNK, dv, ATTN_KEY_CHUNK), BF16),
                        pltpu.VMEM((2, 8, ATTN_Q_UNIT), F32)],
        compiler_params=_params(("parallel", "parallel", "arbitrary")),
        name="attention",
    )(q, q, *ks, *vs)


def _conformer_tail(main_ref, prev_ref, next_ref, cw_ref, cb_ref, lng_ref, lnb_ref, tm):
    i = pl.program_id(1)
    prev = jnp.where(i > 0, prev_ref[...], 0.0)
    nxt = jnp.where(i < pl.num_programs(1) - 1, next_ref[...], 0.0)
    ext = jnp.concatenate([prev, main_ref[...], nxt], axis=0)
    half = CONV_W // 2
    acc = None
    for r in range(8):
        rolled = ext if r == 0 else pltpu.roll(ext, r, 0)
        for a in range(-2, 2):
            tap = half - (8 * a + r)
            if 0 <= tap < CONV_W:
                lo = CONV_HALO - 8 * a
                term = cw_ref[tap:tap + 1, :] * rolled[lo:lo + tm]
                acc = term if acc is None else acc + term
    hc = _layernorm(acc + cb_ref[...], lng_ref[...], lnb_ref[...])
    return hc * _sigmoid(hc)


def _proj_kernel(*refs, conv, tm, w1):
    if conv:
        (x_ref, mod_ref, g_ref, b1_ref, b1p_ref, b1n_ref, att_ref, w_ref,
         cw_ref, cb_ref, lng_ref, lnb_ref, o_ref) = refs
        b1 = _conformer_tail(b1_ref, b1p_ref, b1n_ref, cw_ref, cb_ref, lng_ref, lnb_ref, tm).astype(BF16)
    else:
        x_ref, mod_ref, g_ref, b1_ref, att_ref, w_ref, o_ref = refs
        b1 = b1_ref[...]
    o = _dot(b1, w_ref[:w1, :]) + _dot(att_ref[...], w_ref[w1:, :])
    o_ref[...] = x_ref[...] + mod_ref[2:3, :] * _rms(o, g_ref[...])


def _proj(x, mod, g_post, b1, att, w_out, tm, conv_params=None):
    b, n, d = x.shape
    w1 = b1.shape[-1]
    conv = conv_params is not None
    in_specs = [_tok_spec(tm, d), _mod_spec(mod), _const_spec((1, d)), _tok_spec(tm, w1)]
    args = [x, mod, g_post, b1]
    if conv:
        in_specs += list(_halo_specs(tm, CONV_HALO, n, w1))
        args += [b1, b1]
    in_specs += [_tok_spec(tm, att.shape[-1]), _const_spec(w_out.shape)]
    args += [att, w_out]
    if conv:
        in_specs += [_const_spec(p.shape) for p in conv_params]
        args += list(conv_params)
    return pl.pallas_call(
        functools.partial(_proj_kernel, conv=conv, tm=tm, w1=w1),
        grid=(b, n // tm),
        in_specs=in_specs,
        out_specs=_tok_spec(tm, d),
        out_shape=jax.ShapeDtypeStruct((b, n, d), F32),
        compiler_params=_params(("parallel", "parallel")),
        name="proj_conv" if conv else "proj",
    )(*args)


def _ffn_kernel(x_ref, xp_ref, xn_ref, mod_ref, g2_ref, gp2_ref, win_ref, cw_ref, cb_ref, wout_ref,
                o_ref, act_ref, *, tm):
    i = pl.program_id(1)
    shift = mod_ref[3:4, :]
    scale1 = 1.0 + mod_ref[4:5, :]

    def pre(xx):
        return _rms(xx, g2_ref[...]) * scale1 + shift

    x = x_ref[...]
    hm = pre(x)
    hp = jnp.where(i > 0, pre(xp_ref[...]), 0.0)
    hn = jnp.where(i < pl.num_programs(1) - 1, pre(xn_ref[...]), 0.0)
    hext = jnp.concatenate([hp, hm, hn], axis=0).astype(BF16)
    hmb = hm.astype(BF16)
    rows = tm + 2 * FFN_HALO
    for c in range(D_FF // FFN_CHUNK):
        cols = slice(c * FFN_CHUNK, (c + 1) * FFN_CHUNK)
        zg = _dot(hext, win_ref[:, cols])
        zu = _dot(hmb, win_ref[:, D_FF + c * FFN_CHUNK:D_FF + (c + 1) * FFN_CHUNK])
        g = (cw_ref[0:1, cols] * pltpu.roll(zg, 1, 0)[FFN_HALO:FFN_HALO + tm]
             + cw_ref[1:2, cols] * zg[FFN_HALO:FFN_HALO + tm]
             + cw_ref[2:3, cols] * pltpu.roll(zg, rows - 1, 0)[FFN_HALO:FFN_HALO + tm]
             + cb_ref[:, cols])
        act_ref[:, cols] = (g * _sigmoid(g) * zu).astype(BF16)
    f = _dot(act_ref[...], wout_ref[...])
    o_ref[...] = x + mod_ref[5:6, :] * _rms(f, gp2_ref[...])


def _ffn(x, mod, g_pre2, g_post2, w_in, conv_w, conv_b, w_out, tm):
    b, n, d = x.shape
    prev, nxt = _halo_specs(tm, FFN_HALO, n, d)
    return pl.pallas_call(
        functools.partial(_ffn_kernel, tm=tm),
        grid=(b, n // tm),
        in_specs=[_tok_spec(tm, d), prev, nxt, _mod_spec(mod), _const_spec((1, d)), _const_spec((1, d)),
                  _const_spec(w_in.shape), _const_spec(conv_w.shape), _const_spec(conv_b.shape),
                  _const_spec(w_out.shape)],
        out_specs=_tok_spec(tm, d),
        out_shape=jax.ShapeDtypeStruct((b, n, d), F32),
        scratch_shapes=[pltpu.VMEM((tm, D_FF), BF16)],
        compiler_params=_params(("parallel", "parallel")),
        name="conv_ffn",
    )(x, x, x, mod, g_pre2, g_post2, w_in, conv_w, conv_b, w_out)


def _deinterleave(d):
    return np.concatenate([np.arange(0, d, 2), np.arange(1, d, 2)])


def _rope_angles(n, d_rot):
    t = jnp.arange(n)
    row = (t // GRID_W).astype(F32)
    col = (t % GRID_W).astype(F32)
    n_freq = d_rot // 4
    freq = ROPE_THETA ** (-jnp.arange(n_freq, dtype=F32) / n_freq)
    ang = jnp.concatenate([row[:, None] * freq, col[:, None] * freq], axis=-1)
    return jnp.cos(ang), jnp.sin(ang)


def kernel(x, c, ctx, c_ctx, mod_w, mod_b, norm_g, ffn_w_in, ffn_conv_w, ffn_conv_b, ffn_w_out,
           ev_w_in, ev_a_ln, ev_w_s, ev_b_s, ev_qk_g, ev_w_out,
           od_w_in, od_conv_w, od_conv_b, od_c_ln, od_g_cq, od_g_ckv, od_w_uq, od_w_ukv, od_w_out):
    b, n, d = x.shape
    l_ctx = ctx.shape[1]
    depth = mod_w.shape[0]
    tm_lat, tm_ctx = 512, l_ctx
    tm_in = {"lat": 1024, "ctx": l_ctx}

    n_rows = -(-(b + 1) // 8) * 8
    cc = jnp.concatenate([c, c_ctx[None, :], jnp.zeros((n_rows - b - 1, d), F32)], axis=0)
    mods = _modulation(cc, mod_w, mod_b)

    cos_b, sin_b = _rope_angles(n, B_HEAD_DIM)
    ev_cos = {"lat": jnp.concatenate([cos_b, cos_b], axis=-1), "ctx": jnp.ones((l_ctx, B_HEAD_DIM), F32)}
    ev_sin = {"lat": jnp.concatenate([-sin_b, sin_b], axis=-1), "ctx": jnp.zeros((l_ctx, B_HEAD_DIM), F32)}
    cos_d, sin_d = _rope_angles(n, MLA_ROPE)
    hr = MLA_ROPE // 2
    zpad = jnp.zeros((n, 128 - MLA_ROPE), F32)
    od_cos = {"lat": jnp.concatenate([cos_d, cos_d, zpad], axis=-1),
              "ctx": jnp.concatenate([jnp.ones((l_ctx, MLA_ROPE), F32), jnp.zeros((l_ctx, 128 - MLA_ROPE), F32)], -1)}
    od_sa = {"lat": jnp.concatenate([jnp.zeros((n, hr), F32), sin_d, zpad], axis=-1),
             "ctx": jnp.zeros((l_ctx, 128), F32)}
    od_sb = {"lat": jnp.concatenate([-sin_d, jnp.zeros((n, hr), F32), zpad], axis=-1),
             "ctx": jnp.zeros((l_ctx, 128), F32)}

    perm_b = _deinterleave(B_HEAD_DIM)
    perm_d = _deinterleave(MLA_ROPE)

    xs = {"lat": x, "ctx": ctx}
    tms = {"lat": tm_lat, "ctx": tm_ctx}
    for layer in range(depth):
        more = layer < depth - 1
        i = layer // 2
        mod = {"lat": mods[layer, :b], "ctx": mods[layer, b:b + 1]}
        g_pre1, g_post1, g_pre2, g_post2 = (norm_g[layer, j][None, :] for j in range(4))
        streams = ("lat", "ctx")
        if layer % 2 == 0:
            w = ev_w_in[i]
            wq = w[:, EV_A_END:EV_Q_END].reshape(d, B_HEADS, B_HEAD_DIM)[:, :, perm_b].reshape(d, -1)
            wk = w[:, EV_Q_END:EV_K_END].reshape(d, B_KV_HEADS, B_HEAD_DIM)[:, :, perm_b].reshape(d, -1)
            w_in = jnp.concatenate([w[:, :EV_A_END], wq, wk, w[:, EV_K_END:]], axis=1).astype(BF16)
            ln_g, ln_b = ev_a_ln[i, 0][None, :], ev_a_ln[i, 1][None, :]
            w_s = ev_w_s[i].astype(BF16)
            b_s = jnp.broadcast_to(ev_b_s[i][:, :, None], (A_GROUPS, CHUNK, A_GROUP_DIM))
            gq, gk = ev_qk_g[i, 0][perm_b][None, :], ev_qk_g[i, 1][perm_b][None, :]
            w_out = ev_w_out[i].astype(BF16)
            pr = {s: _ev_in(xs[s], mod[s], g_pre1, w_in, ln_g, ln_b, w_s, b_s, gq, gk,
                            ev_cos[s], ev_sin[s], tm_in[s]) for s in streams}
            geom = dict(nh=B_HEADS // B_KV_HEADS, nkv=1, dq=B_HEAD_DIM, dv=B_HEAD_DIM, tq=1024)
            conv_params = None
        else:
            w = od_w_in[i]
            w_in = jnp.concatenate([w[:, :OD_KV_END], w[:, OD_KV_END:][:, perm_d],
                                    jnp.zeros((d, 128 - MLA_ROPE), F32)], axis=1).astype(BF16)
            wuq = od_w_uq[i].reshape(Q_LORA, MLA_HEADS, MLA_NOPE + MLA_ROPE)
            w_uq = jnp.concatenate([wuq[:, :, :MLA_NOPE], wuq[:, :, MLA_NOPE:][:, :, perm_d],
                                    jnp.zeros((Q_LORA, MLA_HEADS, MLA_QK_PAD - MLA_NOPE - MLA_ROPE), F32)],
                                   axis=-1).reshape(Q_LORA, -1).astype(BF16)
            wukv = od_w_ukv[i].reshape(KV_LORA, MLA_HEADS, MLA_NOPE + MLA_V)
            w_ukv = jnp.concatenate([wukv[:, :, :MLA_NOPE].reshape(KV_LORA, -1),
                                     wukv[:, :, MLA_NOPE:].reshape(KV_LORA, -1)], axis=1).astype(BF16)
            w_out = od_w_out[i].astype(BF16)
            pr = {s: _od_in(xs[s], mod[s], g_pre1, w_in, od_g_cq[i][None, :], od_g_ckv[i][None, :],
                            w_uq, w_ukv, od_cos[s], od_sa[s], od_sb[s], tm_in[s]) for s in streams}
            geom = dict(nh=2, nkv=2, dq=MLA_QK_PAD, dv=MLA_V, tq=2048)
            conv_params = (od_conv_w[i], od_conv_b[i][None, :], od_c_ln[i, 0][None, :], od_c_ln[i, 1][None, :])

        fw_in = ffn_w_in[layer].astype(BF16)
        fw_out = ffn_w_out[layer].astype(BF16)
        fcw, fcb = ffn_conv_w[layer], ffn_conv_b[layer][None, :]
        kv_of = {"lat": ("ctx", "lat"), "ctx": ("ctx",)}
        new_xs = {}
        for s in (streams if more else ("lat",)):
            b1, q = pr[s][0], pr[s][1]
            ks = [pr[t][2] for t in kv_of[s]]
            vs = [pr[t][3] for t in kv_of[s]]
            att = _attention(q, ks, vs, **dict(geom, tq=min(geom["tq"], q.shape[1])))
            x1 = _proj(xs[s], mod[s], g_post1, b1, att, w_out, tms[s], conv_params)
            new_xs[s] = _ffn(x1, mod[s], g_pre2, g_post2, fw_in, fcw, fcb, fw_out, tms[s])
        xs = new_xs
    return xs["lat"]
```
